```python
import math
import jax, jax.numpy as jnp
from jax import lax
import numpy as np

D_MODEL = 1024
BATCH = 2
SEQ = 8192
DEPTH = 2
DEC_BATCH = 128
DEC_SEQ = 1
PAST_LEN = 16384
PAGE_SIZE = 128

MIX_WIDTH = D_MODEL
W_A = MIX_WIDTH // 4
H_A = 4
HD_A = W_A // H_A
CHUNK = 128
W_B = MIX_WIDTH // 2
HEAD_DIM = 64
N_HEADS = W_B // HEAD_DIM
KV_HEADS = 2
GQ = N_HEADS // KV_HEADS
WINDOW = 128
W_C = MIX_WIDTH - W_A - W_B
CONV_W = 3
N_BUCKETS = 32
MAX_EXACT = N_BUCKETS // 2
MAX_DIST = 128
N_EXPERTS = 64
TOP_K = 8
N_GROUPS = 8
TOPK_GROUPS = 4
D_EXPERT = D_MODEL // 4
D_SHARED = D_EXPERT
ROUTE_SCALE = 2.5
MOE_BLOCK = 128
ALPHA = (2 * DEPTH) ** 0.25
BETA = (8 * DEPTH) ** -0.25
LN_EPS = 1e-5
IN_COLS = 2 * W_A + (N_HEADS + 2 * KV_HEADS) * HEAD_DIM + 3 * W_C

kernel_name = 'hymba_gmlp_swa_conv_moe_step'


def layer_norm(x, g, b):
    xf = x.astype(jnp.float32)
    mu = jnp.mean(xf, -1, keepdims=True)
    var = jnp.mean(jnp.square(xf - mu), -1, keepdims=True)
    return ((xf - mu) * lax.rsqrt(var + LN_EPS) * g.astype(jnp.float32) + b.astype(jnp.float32)).astype(x.dtype)


def rms_norm(x, g):
    xf = x.astype(jnp.float32)
    return (xf * lax.rsqrt(jnp.mean(xf * xf, -1, keepdims=True) + LN_EPS) * g.astype(jnp.float32)).astype(x.dtype)


def project_in(x, w_in):
    b, t = x.shape[:2]
    sizes = [W_A, W_A, N_HEADS * HEAD_DIM, KV_HEADS * HEAD_DIM, KV_HEADS * HEAD_DIM, W_C, W_C, W_C]
    cuts = [int(c) for c in np.cumsum(sizes)[:-1]]
    z = jnp.einsum('btd,dc->btc', x, w_in)
    ua, va, q, k, v, gate_b, gate_c, h = jnp.split(z, cuts, axis=-1)
    return (jax.nn.gelu(ua), jax.nn.gelu(va),
            q.reshape(b, t, N_HEADS, HEAD_DIM), k.reshape(b, t, KV_HEADS, HEAD_DIM),
            v.reshape(b, t, KV_HEADS, HEAD_DIM), gate_b, gate_c, h)


def spatial_gating(u, v, g, bn, w_s, b_s):
    b, t = u.shape[:2]
    vn = layer_norm(v, g, bn)
    nc = -(-t // CHUNK)
    vp = jnp.pad(vn, ((0, 0), (0, nc * CHUNK - t), (0, 0))).reshape(b, nc, CHUNK, H_A, HD_A)
    w = jnp.tril(w_s).astype(vp.dtype)
    mixed = jnp.einsum('hts,bcshd->bcthd', w, vp) + b_s.T.astype(vp.dtype)[None, None, :, :, None]
    mixed = mixed.reshape(b, nc * CHUNK, W_A)[:, :t]
    open_rows = t - ((t - 1) // CHUNK) * CHUNK
    return u * mixed, vn[:, t - open_rows:]


def rel_bucket(dist):
    n = jnp.maximum(dist, 0)
    nf = jnp.maximum(n, 1).astype(jnp.float32)
    large = MAX_EXACT + (jnp.log(nf / MAX_EXACT) / math.log(MAX_DIST / MAX_EXACT)
                         * (N_BUCKETS - MAX_EXACT)).astype(jnp.int32)
    large = jnp.minimum(large, N_BUCKETS - 1)
    return jnp.where(n < MAX_EXACT, n, large)


def window_attention(q, k, v, q_pos, k_pos, rel_bias, sinks):
    b, n, nq = q.shape[:3]
    qg = q.reshape(b, n, nq, KV_HEADS, GQ, HEAD_DIM)
    s = jnp.einsum('bnqgrd,bnkgd->bngrqk', qg, k, preferred_element_type=jnp.float32) * (HEAD_DIM ** -0.5)
    dist = q_pos[:, :, None] - k_pos[:, None, :]
    valid = (dist >= 0) & (dist < WINDOW) & (k_pos[:, None, :] >= 0)
    bias = rel_bias.astype(jnp.float32)[rel_bucket(dist)]
    bias = jnp.moveaxis(bias, -1, 1).reshape(n, KV_HEADS, GQ, nq, -1)
    s = jnp.where(valid[:, None, None], s + bias, -jnp.inf)
    sink = jnp.broadcast_to(sinks.astype(jnp.float32).reshape(1, 1, KV_HEADS, GQ, 1, 1), s.shape[:-1] + (1,))
    p = jax.nn.softmax(jnp.concatenate([s, sink], axis=-1), axis=-1)[..., :-1]
    o = jnp.einsum('bngrqk,bnkgd->bnqgrd', p.astype(v.dtype), v)
    return o.reshape(b, n, nq, N_HEADS * HEAD_DIM)


def attn_prompt(q, k, v, rel_bias, sinks):
    b, s = q.shape[:2]
    nb = s // WINDOW
    qb = q.reshape(b, nb, WINDOW, N_HEADS, HEAD_DIM)
    kb = k.reshape(b, nb, WINDOW, KV_HEADS, HEAD_DIM)
    vb = v.reshape(b, nb, WINDOW, KV_HEADS, HEAD_DIM)
    zk = jnp.zeros_like(kb[:, :1])
    kk = jnp.concatenate([jnp.concatenate([zk, kb[:, :-1]], 1), kb], 2)
    vv = jnp.concatenate([jnp.concatenate([zk, vb[:, :-1]], 1), vb], 2)
    pos = jnp.arange(s, dtype=jnp.int32).reshape(nb, WINDOW)
    k_pos = jnp.concatenate([pos - WINDOW, pos], 1)
    o = window_attention(qb, kk, vv, pos, k_pos, rel_bias, sinks)
    return o.reshape(b, s, N_HEADS * HEAD_DIM)


def attn_sample(q, k, v, k_buf, v_buf, rel_bias, sinks):
    t = q.shape[1]
    kk = jnp.concatenate([k_buf.astype(k.dtype), k], 1)
    vv = jnp.concatenate([v_buf.astype(v.dtype), v], 1)
    q_pos = PAST_LEN + jnp.arange(t, dtype=jnp.int32)
    k_pos = PAST_LEN - WINDOW + jnp.arange(WINDOW + t, dtype=jnp.int32)
    o = window_attention(q[:, None], kk[:, None], vv[:, None], q_pos[None], k_pos[None], rel_bias, sinks)
    return o[:, 0], kk[:, -WINDOW:], vv[:, -WINDOW:]


def gated_short_conv(gate_b, gate_c, h, prev, w):
    hc = gate_c * h
    hp = jnp.concatenate([prev.astype(hc.dtype), hc], 1)
    t = h.shape[1]
    y = hp[:, 0:t] * w[:, 0]
    for j in range(1, CONV_W):
        y = y + hp[:, j:j + t] * w[:, j]
    return gate_b * y, hp[:, -(CONV_W - 1):]


def merge_out(ya, yb, yc, merge_g, w_out):
    ga, gb, gc = jnp.split(merge_g, [W_A, W_A + W_B])
    cat = jnp.concatenate([rms_norm(ya, ga), rms_norm(yb, gb), rms_norm(yc, gc)], -1)
    return jnp.einsum('btc,cd->btd', cat, w_out)


def route(xt, router_w, router_bias):
    scores = jax.nn.sigmoid(jnp.einsum('td,de->te', xt, router_w, preferred_element_type=jnp.float32))
    biased = scores + router_bias.astype(jnp.float32)
    grp = biased.reshape(-1, N_GROUPS, N_EXPERTS // N_GROUPS)
    grp_score = jnp.sum(lax.top_k(grp, 2)[0], -1)
    _, grp_idx = lax.top_k(grp_score, TOPK_GROUPS)
    grp_mask = jnp.sum(jax.nn.one_hot(grp_idx, N_GROUPS, dtype=jnp.float32), 1) > 0
    masked = jnp.where(jnp.repeat(grp_mask, N_EXPERTS // N_GROUPS, axis=-1), biased, -jnp.inf)
    _, idx = lax.top_k(masked, TOP_K)
    w = jnp.take_along_axis(scores, idx, -1)
    w = w / jnp.sum(w, -1, keepdims=True) * ROUTE_SCALE
    return idx, w


def routed_experts(xt, idx, gate, w_gate, w_up, w_down):
    a = xt.shape[0] * TOP_K
    flat_e = idx.reshape(a)
    order = jnp.argsort(flat_e)
    e_sorted = flat_e[order]
    tok = order // TOP_K
    counts = jnp.bincount(flat_e, length=N_EXPERTS)
    padded = (counts + MOE_BLOCK - 1) // MOE_BLOCK * MOE_BLOCK
    pad_end = jnp.cumsum(padded)
    pad_start = pad_end - padded
    start = jnp.cumsum(counts) - counts
    dest = pad_start[e_sorted] + jnp.arange(a, dtype=jnp.int32) - start[e_sorted]
    n_blocks = (a + N_EXPERTS * (MOE_BLOCK - 1)) // MOE_BLOCK
    block_e = jnp.searchsorted(pad_end // MOE_BLOCK, jnp.arange(n_blocks, dtype=jnp.int32), side='right')
    block_e = jnp.minimum(block_e, N_EXPERTS - 1)
    xs = jnp.zeros((n_blocks * MOE_BLOCK, xt.shape[1]), xt.dtype).at[dest].set(xt[tok])

    def expert_block(args):
        xb, e = args
        hid = jax.nn.silu(xb @ w_gate[e]) * (xb @ w_up[e])
        return hid @ w_down[e]

    ys = lax.map(expert_block, (xs.reshape(n_blocks, MOE_BLOCK, -1), block_e))
    ys = ys.reshape(n_blocks * MOE_BLOCK, -1)[dest] * gate.reshape(a)[order][:, None].astype(xt.dtype)
    return jnp.zeros_like(xt).at[tok].add(ys)


def moe(x, router_w, router_bias, ew_gate, ew_up, ew_down, sw_gate, sw_up, sw_down):
    b, t, d = x.shape
    xt = x.reshape(b * t, d)
    idx, gate = route(xt, router_w, router_bias)
    shared = (jax.nn.silu(xt @ sw_gate) * (xt @ sw_up)) @ sw_down
    y = routed_experts(xt, idx, gate, ew_gate, ew_up, ew_down) + shared
    return y.reshape(b, t, d)


def residual_block(x, ya, yb, yc, merge_g, w_out, ln1_g, ln1_b, ln2_g, ln2_b, ffn):
    x = layer_norm(ALPHA * x + merge_out(ya, yb, yc, merge_g, w_out), ln1_g, ln1_b)
    return layer_norm(ALPHA * x + moe(x, *ffn), ln2_g, ln2_b)


def setup_inputs(seed: int = 0) -> dict:
    key = jax.random.key(seed)
    ks = iter(jax.random.split(key, 32))

    def nrm(shape, scale):
        return jax.random.normal(next(ks), shape, jnp.float32) * scale

    L = DEPTH
    return {
        'x_prompt': nrm((BATCH, SEQ, D_MODEL), 1.0),
        'x_sample': nrm((DEC_BATCH, DEC_SEQ, D_MODEL), 1.0),
        'cache_k': nrm((L, DEC_BATCH, WINDOW, KV_HEADS, HEAD_DIM), 1.0),
        'cache_v': nrm((L, DEC_BATCH, WINDOW, KV_HEADS, HEAD_DIM), 1.0),
        'state_conv': nrm((L, DEC_BATCH, CONV_W - 1, W_C), 1.0),
        'rel_bias': nrm((N_BUCKETS, N_HEADS), 0.5),
        'w_in': nrm((L, D_MODEL, IN_COLS), D_MODEL ** -0.5),
        'w_out': nrm((L, MIX_WIDTH, D_MODEL), MIX_WIDTH ** -0.5 * BETA),
        'ln_v_g': 1.0 + nrm((L, W_A), 0.05),
        'ln_v_b': nrm((L, W_A), 0.05),
        'sgu_w': nrm((L, H_A, CHUNK, CHUNK), 0.5 * CHUNK ** -0.5),
        'sgu_b': 1.0 + nrm((L, H_A, CHUNK), 0.05),
        'attn_sinks': nrm((L, N_HEADS), 1.0),
        'conv_w': nrm((L, W_C, CONV_W), CONV_W ** -0.5),
        'merge_g': 1.0 + nrm((L, MIX_WIDTH), 0.05),
        'ln1_g': 1.0 + nrm((L, D_MODEL), 0.05),
        'ln1_b': nrm((L, D_MODEL), 0.05),
        'ln2_g': 1.0 + nrm((L, D_MODEL), 0.05),
        'ln2_b': nrm((L, D_MODEL), 0.05),
        'router_w': nrm((L, D_MODEL, N_EXPERTS), D_MODEL ** -0.5),
        'router_bias': nrm((L, N_EXPERTS), 0.01),
        'ew_gate': nrm((L, N_EXPERTS, D_MODEL, D_EXPERT), D_MODEL ** -0.5),
        'ew_up': nrm((L, N_EXPERTS, D_MODEL, D_EXPERT), D_MODEL ** -0.5),
        'ew_down': nrm((L, N_EXPERTS, D_EXPERT, D_MODEL), D_EXPERT ** -0.5 * BETA),
        'sw_gate': nrm((L, D_MODEL, D_SHARED), D_MODEL ** -0.5),
        'sw_up': nrm((L, D_MODEL, D_SHARED), D_MODEL ** -0.5),
        'sw_down': nrm((L, D_SHARED, D_MODEL), D_SHARED ** -0.5 * BETA),
    }


def reference(x_prompt, x_sample, cache_k, cache_v, state_conv, rel_bias, w_in, w_out, ln_v_g, ln_v_b,
              sgu_w, sgu_b, attn_sinks, conv_w, merge_g, ln1_g, ln1_b, ln2_g, ln2_b, router_w, router_bias,
              ew_gate, ew_up, ew_down, sw_gate, sw_up, sw_down):
    xp, xs = x_prompt, x_sample
    pk, pv, pc, ps, sk, sv, sc, ss = [], [], [], [], [], [], [], []
    for l in range(DEPTH):
        ffn = (router_w[l], router_bias[l], ew_gate[l], ew_up[l], ew_down[l], sw_gate[l], sw_up[l], sw_down[l])
        ua, va, q, k, v, gate_b, gate_c, h = project_in(xp, w_in[l])
        ya, vrows = spatial_gating(ua, va, ln_v_g[l], ln_v_b[l], sgu_w[l], sgu_b[l])
        yb = attn_prompt(q, k, v, rel_bias, attn_sinks[l])
        zero_prev = jnp.zeros((xp.shape[0], CONV_W - 1, W_C), h.dtype)
        yc, conv_tail = gated_short_conv(gate_b, gate_c, h, zero_prev, conv_w[l])
        xp = residual_block(xp, ya, yb, yc, merge_g[l], w_out[l], ln1_g[l], ln1_b[l], ln2_g[l], ln2_b[l], ffn)
        pk.append(k[:, -WINDOW:])
        pv.append(v[:, -WINDOW:])
        pc.append(conv_tail)
        ps.append(vrows)
        ua, va, q, k, v, gate_b, gate_c, h = project_in(xs, w_in[l])
        ya, vrows = spatial_gating(ua, va, ln_v_g[l], ln_v_b[l], sgu_w[l], sgu_b[l])
        yb, k_new, v_new = attn_sample(q, k, v, cache_k[l], cache_v[l], rel_bias, attn_sinks[l])
        yc, conv_tail = gated_short_conv(gate_b, gate_c, h, state_conv[l], conv_w[l])
        xs = residual_block(xs, ya, yb, yc, merge_g[l], w_out[l], ln1_g[l], ln1_b[l], ln2_g[l], ln2_b[l], ffn)
        sk.append(k_new)
        sv.append(v_new)
        sc.append(conv_tail)
        ss.append(vrows)
    y_prompt, y_sample = xp, xs
    prompt_k, prompt_v, prompt_conv, prompt_sgu_v = jnp.stack(pk), jnp.stack(pv), jnp.stack(pc), jnp.stack(ps)
    sample_k, sample_v, sample_conv, sample_sgu_v = jnp.stack(sk), jnp.stack(sv), jnp.stack(sc), jnp.stack(ss)
    return (y_prompt, y_sample, prompt_k, prompt_v, prompt_conv, prompt_sgu_v, sample_k, sample_v, sample_conv, sample_sgu_v)
```

```python
import functools
import math

import numpy as np
import jax
import jax.numpy as jnp
from jax import lax
from jax.experimental import pallas as pl
from jax.experimental.pallas import tpu as pltpu

D_MODEL = 1024
BATCH = 2
SEQ = 8192
DEPTH = 2
DEC_BATCH = 128
PAST_LEN = 16384
W_A = 256
H_A = 4
HD_A = 64
CHUNK = 128
W_B = 512
HEAD_DIM = 64
N_HEADS = 8
KV_HEADS = 2
GQ = 4
WINDOW = 128
W_C = 256
CONV_W = 3
N_BUCKETS = 32
MAX_EXACT = 16
MAX_DIST = 128
N_EXPERTS = 64
TOP_K = 8
N_GROUPS = 8
TOPK_GROUPS = 4
GROUP_SIZE = N_EXPERTS // N_GROUPS
D_EXPERT = 256
ROUTE_SCALE = 2.5
ALPHA = (2 * DEPTH) ** 0.25
LN_EPS = 1e-5
IN_COLS = 2048

T_PROMPT = BATCH * SEQ
T_ALL = T_PROMPT + DEC_BATCH

LANES = 128
SUBLANES = 8
VMEM_LIMIT = 56 * 1024 * 1024

TM = 512
NBLK = TM // WINDOW
TILES_PER_SEQ = SEQ // TM
ROWS_F32 = D_MODEL // LANES
ROWS_PK = ROWS_F32 // 2
N_TGROUPS = 4
TG = T_ALL // N_TGROUPS
RB = 128
NEG = -1e30

BF = jnp.bfloat16
F32 = jnp.float32


def _ln(x, g, b):
    mu = jnp.mean(x, -1, keepdims=True)
    xc = x - mu
    var = jnp.mean(xc * xc, -1, keepdims=True)
    return xc * lax.rsqrt(var + LN_EPS) * g + b


def _rms(x, g):
    return x * lax.rsqrt(jnp.mean(x * x, -1, keepdims=True) + LN_EPS) * g


def _dot(a, b):
    return jnp.dot(a, b, preferred_element_type=F32)


def _dot_nt(a, b):
    return lax.dot_general(a, b, (((1,), (1,)), ((), ())), preferred_element_type=F32)


def _pack_pairs(lo, hi):
    lo_u = pltpu.bitcast(lo.astype(BF).astype(F32), jnp.uint32)
    hi_u = pltpu.bitcast(hi.astype(BF).astype(F32), jnp.uint32)
    return (hi_u & jnp.uint32(0xFFFF0000)) | (lo_u >> 16)


def _route(logits_t, rbias):
    n = logits_t.shape[1]
    scores = jax.nn.sigmoid(logits_t)
    biased = scores + rbias
    g3 = biased.reshape(N_GROUPS, GROUP_SIZE, n)
    sub = lax.broadcasted_iota(jnp.int32, g3.shape, 1)
    m1 = jnp.max(g3, axis=1, keepdims=True)
    first = jnp.min(jnp.where(g3 == m1, sub, GROUP_SIZE), axis=1, keepdims=True)
    m2 = jnp.max(jnp.where(sub == first, -jnp.inf, g3), axis=1, keepdims=True)
    gs = (m1 + m2).reshape(N_GROUPS, n)
    gi = lax.broadcasted_iota(jnp.int32, gs.shape, 0)
    grank = jnp.zeros(gs.shape, jnp.int32)
    for j in range(N_GROUPS):
        row = gs[j:j + 1]
        grank = grank + ((row > gs) | ((row == gs) & (j < gi))).astype(jnp.int32)
    gsel = (grank < TOPK_GROUPS).reshape(N_GROUPS, 1, n)
    masked = jnp.where(gsel, g3, -jnp.inf).reshape(N_EXPERTS, n)
    ei = lax.broadcasted_iota(jnp.int32, masked.shape, 0)
    rank = jnp.zeros(masked.shape, jnp.int32)
    for j in range(N_EXPERTS):
        row = masked[j:j + 1]
        rank = rank + ((row > masked) | ((row == masked) & (j < ei))).astype(jnp.int32)
    sel = rank < TOP_K
    w = jnp.where(sel, scores, 0.0)
    return w / jnp.sum(w, axis=0, keepdims=True) * ROUTE_SCALE


def _post_mixers(x, ya, yb, yc, mg_ref, wout_ref, ln1g_ref, ln1b_ref, wrt_ref, rb_ref,
                 swgu_ref, swd_ref, base_ref, xp_ref, gw_ref):
    n = x.shape[0]
    mg = mg_ref[...]
    cat = jnp.concatenate([_rms(ya, mg[:, :W_A]), _rms(yb, mg[:, W_A:W_A + W_B]),
                           _rms(yc, mg[:, W_A + W_B:])], axis=-1)
    m = _dot(cat.astype(BF), wout_ref[...])
    x1 = _ln(ALPHA * x + m, ln1g_ref[...], ln1b_ref[...])
    x1b = x1.astype(BF)
    gw_ref[...] = _route(_dot_nt(wrt_ref[...], x1b), rb_ref[...])
    hg = _dot(x1b, swgu_ref[...])
    hs = jax.nn.silu(hg[:, :D_EXPERT]) * hg[:, D_EXPERT:]
    base = ALPHA * x1 + _dot(hs.astype(BF), swd_ref[...])
    for c in range(ROWS_F32):
        base_ref[pl.ds(c, n, stride=ROWS_F32), :] = base[:, c * LANES:(c + 1) * LANES]
    half = D_MODEL // 2
    for j in range(ROWS_PK):
        xp_ref[pl.ds(j, n, stride=ROWS_PK), :] = _pack_pairs(
            x1[:, j * LANES:(j + 1) * LANES], x1[:, half + j * LANES:half + (j + 1) * LANES])


def _load_tokens(first_layer, xin_ref, n, lng_ref, lnb_ref):
    if first_layer:
        return xin_ref[...]
    s = jnp.concatenate([xin_ref[pl.ds(c, n, stride=ROWS_F32), :] for c in range(ROWS_F32)], axis=-1)
    return _ln(s, lng_ref[...], lnb_ref[...])


def _head_mask_sum(r):
    lane = lax.broadcasted_iota(jnp.int32, (CHUNK, W_A), 1) // HD_A
    out = r[0:CHUNK]
    for h in range(1, H_A):
        out = jnp.where(lane == h, r[h * CHUNK:(h + 1) * CHUNK], out)
    return out


def _prompt_mixer_kernel(first_layer, sinks_ref, *refs):
    (xin_ref, lng_ref, lnb_ref, win_ref, wout_ref, sguw_ref,
     sgub_ref, lnvg_ref, lnvb_ref, bprev_ref, bcur_ref, convw_ref, mg_ref, ln1g_ref,
     ln1b_ref, wrt_ref, rb_ref, swgu_ref, swd_ref,
     base_ref, xp_ref, gw_ref, klast_ref, vlast_ref, ctail_ref, vrows_ref,
     kprev_s, vprev_s, hcprev_s) = refs[(3 if first_layer else 2):]
    t = pl.program_id(1)
    is_first = t == 0

    @pl.when(is_first)
    def _():
        kprev_s[...] = jnp.zeros_like(kprev_s)
        vprev_s[...] = jnp.zeros_like(vprev_s)
        hcprev_s[...] = jnp.zeros_like(hcprev_s)

    x = _load_tokens(first_layer, xin_ref, TM, lng_ref, lnb_ref)
    z = _dot(x.astype(BF), win_ref[...])
    ua = jax.nn.gelu(z[:, 0:W_A])
    vn = _ln(jax.nn.gelu(z[:, W_A:2 * W_A]), lnvg_ref[...], lnvb_ref[...])
    q = z[:, 512:1024]
    k = z[:, 1024:1152]
    v = z[:, 1152:1280]
    gate_b = z[:, 1280:1536]
    gate_c = z[:, 1536:1792]
    h = z[:, 1792:2048]

    vnb = vn.astype(BF)
    sgub = sgub_ref[...]
    mixed = [_head_mask_sum(_dot(sguw_ref[...], vnb[c * CHUNK:(c + 1) * CHUNK])) + sgub for c in range(NBLK)]
    ya = ua * jnp.concatenate(mixed, axis=0)

    kb = k.astype(BF)
    vb = v.astype(BF)
    qb = q.astype(BF)
    scale = HEAD_DIM ** -0.5
    yb_blocks = []
    for c in range(NBLK):
        if c == 0:
            kp, vp = kprev_s[...], vprev_s[...]
            bsel = jnp.where(is_first, 1, 0)
        else:
            kp, vp = kb[(c - 1) * WINDOW:c * WINDOW], vb[(c - 1) * WINDOW:c * WINDOW]
            bsel = 0
        kc, vc = kb[c * WINDOW:(c + 1) * WINDOW], vb[c * WINDOW:(c + 1) * WINDOW]
        qc = qb[c * WINDOW:(c + 1) * WINDOW]
        outs = [None] * N_HEADS
        for g in range(KV_HEADS):
            gs = slice(g * HEAD_DIM, (g + 1) * HEAD_DIM)
            qg = jnp.concatenate([qc[:, (2 * r + g) * HEAD_DIM:(2 * r + g + 1) * HEAD_DIM] for r in range(GQ)], axis=0)
            sp = _dot_nt(qg, kp[:, gs]) * scale + bprev_ref[bsel, g]
            sc = _dot_nt(qg, kc[:, gs]) * scale + bcur_ref[g]
            for r in range(GQ):
                rs = slice(r * WINDOW, (r + 1) * WINDOW)
                sink = sinks_ref[g * GQ + r]
                spr, scr = sp[rs], sc[rs]
                mx = jnp.maximum(jnp.maximum(jnp.max(spr, -1, keepdims=True), jnp.max(scr, -1, keepdims=True)), sink)
                ep, ec = jnp.exp(spr - mx), jnp.exp(scr - mx)
                den = jnp.sum(ep, -1, keepdims=True) + jnp.sum(ec, -1, keepdims=True) + jnp.exp(sink - mx)
                o = _dot((ep / den).astype(BF), vp[:, gs]) + _dot((ec / den).astype(BF), vc[:, gs])
                outs[2 * r + g] = o
        yb_blocks.append(jnp.concatenate(outs, axis=-1))
    yb = jnp.concatenate(yb_blocks, axis=0)
    kprev_s[...] = kb[TM - WINDOW:]
    vprev_s[...] = vb[TM - WINDOW:]

    hc = gate_c * h
    prev = hcprev_s[...]
    row = lax.broadcasted_iota(jnp.int32, hc.shape, 0)
    h1 = jnp.where(row == 0, prev[7:8], pltpu.roll(hc, 1, 0))
    h2 = jnp.where(row == 0, prev[6:7], jnp.where(row == 1, prev[7:8], pltpu.roll(hc, 2, 0)))
    cw = convw_ref[...]
    yc = gate_b * (h2 * cw[0:1] + h1 * cw[1:2] + hc * cw[2:3])
    hcprev_s[...] = hc[TM - SUBLANES:]

    @pl.when(t == TILES_PER_SEQ - 1)
    def _():
        klast_ref[0] = k[TM - WINDOW:]
        vlast_ref[0] = v[TM - WINDOW:]
        ctail_ref[0] = hc[TM - (CONV_W - 1):]
        vrows_ref[0] = vn[TM - CHUNK:]

    _post_mixers(x, ya, yb, yc, mg_ref, wout_ref, ln1g_ref, ln1b_ref, wrt_ref, rb_ref, swgu_ref, swd_ref,
                 base_ref, xp_ref, gw_ref)


def _full(shape):
    nd = len(shape)
    return pl.BlockSpec(shape, lambda *_: (0,) * nd)


def _prompt_mixer(first_layer, xin, p, donors):
    nt = TILES_PER_SEQ
    tok = lambda b, t, *_: (b * nt + t, 0)
    anyspec = pl.BlockSpec(memory_space=pl.ANY)
    if first_layer:
        xin_spec = pl.BlockSpec((TM, D_MODEL), tok)
        aliases = {1: 0, 2: 1, 3: 2}
    else:
        xin_spec = pl.BlockSpec((TM * ROWS_F32, LANES), tok)
        aliases = {3: 0, 1: 1, 2: 2}
    in_specs = [anyspec] * len(donors) + [
        xin_spec, _full((1, D_MODEL)), _full((1, D_MODEL)),
        _full((D_MODEL, IN_COLS)), _full((D_MODEL, D_MODEL)), _full((H_A * CHUNK, CHUNK)),
        _full((CHUNK, W_A)), _full((1, W_A)), _full((1, W_A)),
        _full((2, KV_HEADS, GQ * WINDOW, WINDOW)), _full((KV_HEADS, GQ * WINDOW, WINDOW)),
        _full((CONV_W, W_C)), _full((1, D_MODEL)), _full((1, D_MODEL)), _full((1, D_MODEL)),
        _full((N_EXPERTS, D_MODEL)), _full((N_EXPERTS, 1)), _full((D_MODEL, 2 * D_EXPERT)),
        _full((D_EXPERT, D_MODEL)),
    ]
    per_seq = lambda b, t, *_: (b, 0, 0)
    out_shape = [
        jax.ShapeDtypeStruct((T_ALL * ROWS_F32, LANES), F32),
        jax.ShapeDtypeStruct((T_ALL * ROWS_PK, LANES), jnp.uint32),
        jax.ShapeDtypeStruct((N_EXPERTS, T_ALL), F32),
        jax.ShapeDtypeStruct((BATCH, WINDOW, KV_HEADS * HEAD_DIM), F32),
        jax.ShapeDtypeStruct((BATCH, WINDOW, KV_HEADS * HEAD_DIM), F32),
        jax.ShapeDtypeStruct((BATCH, CONV_W - 1, W_C), F32),
        jax.ShapeDtypeStruct((BATCH, CHUNK, W_A), F32),
    ]
    out_specs = [
        pl.BlockSpec((TM * ROWS_F32, LANES), tok),
        pl.BlockSpec((TM * ROWS_PK, LANES), tok),
        pl.BlockSpec((N_EXPERTS, TM), lambda b, t, *_: (0, b * nt + t)),
        pl.BlockSpec((1, WINDOW, KV_HEADS * HEAD_DIM), per_seq),
        pl.BlockSpec((1, WINDOW, KV_HEADS * HEAD_DIM), per_seq),
        pl.BlockSpec((1, CONV_W - 1, W_C), per_seq),
        pl.BlockSpec((1, CHUNK, W_A), per_seq),
    ]
    return pl.pallas_call(
        functools.partial(_prompt_mixer_kernel, first_layer),
        grid_spec=pltpu.PrefetchScalarGridSpec(
            num_scalar_prefetch=1, grid=(BATCH, nt), in_specs=in_specs, out_specs=out_specs,
            scratch_shapes=[pltpu.VMEM((WINDOW, KV_HEADS * HEAD_DIM), BF),
                            pltpu.VMEM((WINDOW, KV_HEADS * HEAD_DIM), BF),
                            pltpu.VMEM((SUBLANES, W_C), F32)]),
        out_shape=out_shape,
        input_output_aliases=aliases,
        compiler_params=pltpu.CompilerParams(dimension_semantics=("arbitrary", "arbitrary"),
                                             vmem_limit_bytes=VMEM_LIMIT),
        name="prompt_mixer",
    )(p["sinks"], *donors, xin, p["ln_prev_g"], p["ln_prev_b"], p["w_in"], p["w_out"], p["sgu_w"], p["sgu_b"],
      p["ln_v_g"], p["ln_v_b"], p["bias_prev"], p["bias_cur"], p["conv_w"], p["merge_g"], p["ln1_g"],
      p["ln1_b"], p["router_wt"], p["router_b"], p["sw_gu"], p["sw_d"])


def _half_sums(a, lane_lo):
    s0 = jnp.sum(jnp.where(lane_lo, a, 0.0), axis=1, keepdims=True)
    s1 = jnp.sum(jnp.where(lane_lo, 0.0, a), axis=1, keepdims=True)
    return jnp.where(lane_lo, s0, s1)


def _sample_mixer_kernel(first_layer, *refs):
    (xin_ref, lng_ref, lnb_ref, win_ref, wout_ref,
     sguw0_ref, sgub0_ref, lnvg_ref, lnvb_ref, ck_ref, cv_ref, bsamp_ref, bnew_ref,
     sinkrow_ref, s0_ref, s1_ref, convw_ref, mg_ref, ln1g_ref, ln1b_ref, wrt_ref,
     rb_ref, swgu_ref, swd_ref,
     base_ref, xp_ref, gw_ref, knew_ref, vnew_ref, hc_ref, vn_ref,
     q_s, yb_s, kn_s, vn_s) = refs[(3 if first_layer else 2):]
    n = DEC_BATCH
    x = _load_tokens(first_layer, xin_ref, n, lng_ref, lnb_ref)
    z = _dot(x.astype(BF), win_ref[...])
    ua = jax.nn.gelu(z[:, 0:W_A])
    vn = _ln(jax.nn.gelu(z[:, W_A:2 * W_A]), lnvg_ref[...], lnvb_ref[...])
    vn_ref[...] = vn
    ya = ua * (vn.astype(BF).astype(F32) * sguw0_ref[...] + sgub0_ref[...])
    k = z[:, 1024:1152]
    v = z[:, 1152:1280]
    knew_ref[...] = k
    vnew_ref[...] = v
    q_s[...] = z[:, 512:1024].astype(BF).astype(F32)
    kn_s[...] = k.astype(BF).astype(F32)
    vn_s[...] = v.astype(BF).astype(F32)
    scale = HEAD_DIM ** -0.5
    lane_lo = lax.broadcasted_iota(jnp.int32, (1, LANES), 1) < HEAD_DIM

    def per_seq_tile(bi, carry):
        b0 = pl.multiple_of(bi * SUBLANES, SUBLANES)
        q8 = q_s[pl.ds(b0, SUBLANES), :]
        kn8 = kn_s[pl.ds(b0, SUBLANES), :]
        vn8 = vn_s[pl.ds(b0, SUBLANES), :]
        rows = []
        for u in range(SUBLANES):
            kb = ck_ref[b0 + u].astype(F32)
            vb = cv_ref[b0 + u].astype(F32)
            kn = kn8[u:u + 1]
            vnw = vn8[u:u + 1]
            pieces = []
            for r in range(GQ):
                qp = q8[u:u + 1, r * LANES:(r + 1) * LANES]
                s = _half_sums(kb * qp, lane_lo) * scale + bsamp_ref[r]
                sn = _half_sums(kn * qp, lane_lo) * scale + bnew_ref[r]
                sink = sinkrow_ref[r]
                mx = jnp.maximum(jnp.maximum(jnp.max(s, axis=0, keepdims=True), sn), sink)
                e = jnp.exp(s - mx)
                en = jnp.exp(sn - mx)
                den = jnp.sum(e, axis=0, keepdims=True) + en + jnp.exp(sink - mx)
                p = (e / den).astype(BF).astype(F32)
                pn = (en / den).astype(BF).astype(F32)
                pieces.append(jnp.sum(p * vb, axis=0, keepdims=True) + pn * vnw)
            rows.append(jnp.concatenate(pieces, axis=1))
        yb_s[pl.ds(b0, SUBLANES), :] = jnp.concatenate(rows, axis=0)
        return carry

    lax.fori_loop(0, n // SUBLANES, per_seq_tile, 0)
    yb = yb_s[...]

    hc = z[:, 1536:1792] * z[:, 1792:2048]
    hc_ref[...] = hc
    cw = convw_ref[...]
    yc = z[:, 1280:1536] * (s0_ref[...] * cw[0:1] + s1_ref[...] * cw[1:2] + hc * cw[2:3])
    _post_mixers(x, ya, yb, yc, mg_ref, wout_ref, ln1g_ref, ln1b_ref, wrt_ref, rb_ref, swgu_ref, swd_ref,
                 base_ref, xp_ref, gw_ref)


def _sample_mixer(first_layer, xin, base, xp, gw, p, ck, cv, s0, s1):
    n = DEC_BATCH
    tail = T_PROMPT // n
    anyspec = pl.BlockSpec(memory_space=pl.ANY)
    if first_layer:
        xin_spec = pl.BlockSpec((n, D_MODEL), lambda i: (0, 0))
        lead, lead_specs, aliases = (base, xp, gw, xin), [anyspec] * 3, {0: 0, 1: 1, 2: 2}
    else:
        xin_spec = pl.BlockSpec((n * ROWS_F32, LANES), lambda i: (tail, 0))
        lead, lead_specs, aliases = (xp, gw, base), [anyspec] * 2, {2: 0, 0: 1, 1: 2}
    in_specs = lead_specs + [
        xin_spec, _full((1, D_MODEL)), _full((1, D_MODEL)),
        _full((D_MODEL, IN_COLS)), _full((D_MODEL, D_MODEL)), _full((1, W_A)), _full((1, W_A)),
        _full((1, W_A)), _full((1, W_A)),
        _full((n, WINDOW, LANES)), _full((n, WINDOW, LANES)),
        _full((GQ, WINDOW, LANES)), _full((GQ, 1, LANES)), _full((GQ, 1, LANES)),
        _full((n, W_C)), _full((n, W_C)),
        _full((CONV_W, W_C)), _full((1, D_MODEL)), _full((1, D_MODEL)), _full((1, D_MODEL)),
        _full((N_EXPERTS, D_MODEL)), _full((N_EXPERTS, 1)), _full((D_MODEL, 2 * D_EXPERT)),
        _full((D_EXPERT, D_MODEL)),
    ]
    out_shape = [
        jax.ShapeDtypeStruct(base.shape, base.dtype),
        jax.ShapeDtypeStruct(xp.shape, xp.dtype),
        jax.ShapeDtypeStruct(gw.shape, gw.dtype),
        jax.ShapeDtypeStruct((n, LANES), F32),
        jax.ShapeDtypeStruct((n, LANES), F32),
        jax.ShapeDtypeStruct((n, W_C), F32),
        jax.ShapeDtypeStruct((n, W_A), F32),
    ]
    out_specs = [
        pl.BlockSpec((n * ROWS_F32, LANES), lambda i: (tail, 0)),
        pl.BlockSpec((n * ROWS_PK, LANES), lambda i: (tail, 0)),
        pl.BlockSpec((N_EXPERTS, n), lambda i: (0, tail)),
        _full((n, LANES)), _full((n, LANES)), _full((n, W_C)), _full((n, W_A)),
    ]
    return pl.pallas_call(
        functools.partial(_sample_mixer_kernel, first_layer),
        grid=(1,), in_specs=in_specs, out_specs=out_specs, out_shape=out_shape,
        scratch_shapes=[pltpu.VMEM((n, W_B), F32), pltpu.VMEM((n, W_B), F32),
                        pltpu.VMEM((n, LANES), F32), pltpu.VMEM((n, LANES), F32)],
        input_output_aliases=aliases,
        compiler_params=pltpu.CompilerParams(dimension_semantics=("arbitrary",),
                                             vmem_limit_bytes=VMEM_LIMIT),
        name="sample_mixer",
    )(*lead, p["ln_prev_g"], p["ln_prev_b"], p["w_in"], p["w_out"], p["sgu_w0"], p["sgu_b0"],
      p["ln_v_g"], p["ln_v_b"], ck, cv, p["bias_samp"], p["bias_new"], p["sink_row"], s0, s1,
      p["conv_w"], p["merge_g"], p["ln1_g"], p["ln1_b"], p["router_wt"], p["router_b"], p["sw_gu"],
      p["sw_d"])


LIST_LEN = -(-(TG * TOP_K + RB) // 1024) * 1024
UNROLL = 8


def _expert_kernel(off_ref, tok_ref, gate_ref, xp_ref, wg_ref, wu_ref, wdn_ref, base_any, out_any,
                   acc, xs, xbf, ybuf, wgu_s, wd_s, sem):
    g = pl.program_id(0)
    e = pl.program_id(1)
    rows = TG * ROWS_F32

    def group_copy(to_vmem):
        hbm = (base_any if to_vmem else out_any).at[pl.ds(g * rows, rows)]
        vm = acc.at[pl.ds(0, rows)]
        return pltpu.make_async_copy(hbm, vm, sem) if to_vmem else pltpu.make_async_copy(vm, hbm, sem)

    @pl.when(e == 0)
    def _():
        cp = group_copy(True)
        cp.start()
        acc[pl.ds(rows, ROWS_F32), :] = jnp.zeros((ROWS_F32, LANES), F32)
        xs[...] = jnp.zeros_like(xs)
        cp.wait()

    wgu_s[:, :D_EXPERT] = wg_ref[0].astype(BF)
    wgu_s[:, D_EXPERT:] = wu_ref[0].astype(BF)
    wd_s[...] = wdn_ref[0].astype(BF)

    start = off_ref[g * (N_EXPERTS + 1) + e]
    end = off_ref[g * (N_EXPERTS + 1) + e + 1]
    nblocks = (end - start + RB - 1) // RB

    def block(bi, carry):
        r0 = start + bi * RB

        def gather(i, c):
            for u in range(UNROLL):
                r = i * UNROLL + u
                src = pl.multiple_of(tok_ref[r0 + r] * ROWS_PK, ROWS_PK)
                dst = pl.multiple_of(r * ROWS_PK, ROWS_PK)
                xs[pl.ds(dst, ROWS_PK), :] = xp_ref[pl.ds(src, ROWS_PK), :]
            return c

        lax.fori_loop(0, RB // UNROLL, gather, 0)
        half = D_MODEL // 2
        for j in range(ROWS_PK):
            u32 = xs[pl.ds(j, RB, stride=ROWS_PK), :]
            xbf[:, j * LANES:(j + 1) * LANES] = pltpu.bitcast(u32 << 16, F32).astype(BF)
            xbf[:, half + j * LANES:half + (j + 1) * LANES] = pltpu.bitcast(
                u32 & jnp.uint32(0xFFFF0000), F32).astype(BF)
        hg = _dot(xbf[...], wgu_s[...])
        hid = jax.nn.silu(hg[:, :D_EXPERT]) * hg[:, D_EXPERT:]
        y = _dot(hid.astype(BF), wd_s[...])
        for c in range(ROWS_F32):
            ybuf[pl.ds(c, RB, stride=ROWS_F32), :] = y[:, c * LANES:(c + 1) * LANES]

        def scatter(i, c):
            dsts, vals = [], []
            for u in range(UNROLL):
                r = i * UNROLL + u
                valid = r0 + r < end
                tk = jnp.where(valid, tok_ref[r0 + r], TG)
                gt = jnp.where(valid, gate_ref[r0 + r], 0.0)
                dst = pl.multiple_of(tk * ROWS_F32, ROWS_F32)
                src = pl.multiple_of(r * ROWS_F32, ROWS_F32)
                dsts.append(dst)
                vals.append(acc[pl.ds(dst, ROWS_F32), :] + gt * ybuf[pl.ds(src, ROWS_F32), :])
            for dst, val in zip(dsts, vals):
                acc[pl.ds(dst, ROWS_F32), :] = val
            return c

        lax.fori_loop(0, RB // UNROLL, scatter, 0)
        return carry

    lax.fori_loop(0, nblocks, block, 0)

    @pl.when(e == N_EXPERTS - 1)
    def _():
        cp = group_copy(False)
        cp.start()
        cp.wait()


def _experts(off, tok, gate, xp, wg, wu, wdn, base):
    in_specs = [
        pl.BlockSpec((LIST_LEN,), lambda g, e, *_: (g,), memory_space=pltpu.SMEM),
        pl.BlockSpec((LIST_LEN,), lambda g, e, *_: (g,), memory_space=pltpu.SMEM),
        pl.BlockSpec((TG * ROWS_PK, LANES), lambda g, e, *_: (g, 0)),
        pl.BlockSpec((1, D_MODEL, D_EXPERT), lambda g, e, *_: (e, 0, 0)),
        pl.BlockSpec((1, D_MODEL, D_EXPERT), lambda g, e, *_: (e, 0, 0)),
        pl.BlockSpec((1, D_EXPERT, D_MODEL), lambda g, e, *_: (e, 0, 0)),
        pl.BlockSpec(memory_space=pl.ANY),
    ]
    return pl.pallas_call(
        _expert_kernel,
        grid_spec=pltpu.PrefetchScalarGridSpec(
            num_scalar_prefetch=1, grid=(N_TGROUPS, N_EXPERTS), in_specs=in_specs,
            out_specs=pl.BlockSpec(memory_space=pl.ANY),
            scratch_shapes=[pltpu.VMEM(((TG + 1) * ROWS_F32, LANES), F32),
                            pltpu.VMEM((RB * ROWS_PK, LANES), jnp.uint32),
                            pltpu.VMEM((RB, D_MODEL), BF),
                            pltpu.VMEM((RB * ROWS_F32, LANES), F32),
                            pltpu.VMEM((D_MODEL, 2 * D_EXPERT), BF),
                            pltpu.VMEM((D_EXPERT, D_MODEL), BF),
                            pltpu.SemaphoreType.DMA]),
        out_shape=jax.ShapeDtypeStruct(base.shape, base.dtype),
        input_output_aliases={7: 0},
        compiler_params=pltpu.CompilerParams(dimension_semantics=("arbitrary", "arbitrary"),
                                             vmem_limit_bytes=VMEM_LIMIT),
        name="experts",
    )(off, tok, gate, xp, wg, wu, wdn, base)


def _dispatch_lists(gw):
    chosen = (gw > 0).reshape(N_EXPERTS, N_TGROUPS, TG).transpose(1, 0, 2)
    counts = jnp.sum(chosen, axis=-1, dtype=jnp.int32)
    off = jnp.concatenate([jnp.zeros((N_TGROUPS, 1), jnp.int32), jnp.cumsum(counts, axis=1)], axis=1)
    flat = chosen.reshape(N_TGROUPS, N_EXPERTS * TG)
    idx = jax.vmap(lambda f: jnp.nonzero(f, size=TG * TOP_K, fill_value=0)[0])(flat).astype(jnp.int32)
    gflat = gw.reshape(N_EXPERTS, N_TGROUPS, TG).transpose(1, 0, 2).reshape(N_TGROUPS, N_EXPERTS * TG)
    gate = jnp.take_along_axis(gflat, idx, axis=1)
    pad = ((0, 0), (0, LIST_LEN - TG * TOP_K))
    tok = jnp.pad(idx % TG, pad).reshape(-1)
    return off.reshape(-1), tok, jnp.pad(gate, pad).reshape(-1)


def _final_ln_kernel(n, s_ref, g_ref, b_ref, o_ref):
    o_ref[...] = _load_tokens(False, s_ref, n, g_ref, b_ref)


def _final_ln(s2, g, b, n, steps, first_block):
    return pl.pallas_call(
        functools.partial(_final_ln_kernel, n),
        grid=(steps,),
        in_specs=[pl.BlockSpec((n * ROWS_F32, LANES), lambda i: (first_block + i, 0)),
                  _full((1, D_MODEL)), _full((1, D_MODEL))],
        out_specs=pl.BlockSpec((n, D_MODEL), lambda i: (i, 0)),
        out_shape=jax.ShapeDtypeStruct((n * steps, D_MODEL), F32),
        compiler_params=pltpu.CompilerParams(dimension_semantics=("arbitrary",)),
        name="final_norm",
    )(s2, g, b)


_HEAD_AT = [(p % 2) * GQ + p // 2 for p in range(N_HEADS)]


def _bucket(dist):
    n = np.maximum(dist, 0)
    nf = np.maximum(n, 1).astype(np.float32)
    large = MAX_EXACT + (np.log(nf / np.float32(MAX_EXACT)) / np.float32(math.log(MAX_DIST / MAX_EXACT))
                         * np.float32(N_BUCKETS - MAX_EXACT)).astype(np.int32)
    return np.where(n < MAX_EXACT, n, np.minimum(large, N_BUCKETS - 1)).astype(np.int32)


def _bias_tables(rel_bias):
    qi = np.arange(WINDOW)[:, None]
    kk = np.arange(WINDOW)[None, :]

    def table(dist):
        valid = (dist >= 0) & (dist < WINDOW)
        t = jnp.where(valid[..., None], rel_bias[_bucket(dist)], NEG)
        return jnp.moveaxis(t, -1, 0).reshape(KV_HEADS, GQ * WINDOW, WINDOW)

    prev = table(WINDOW + qi - kk)
    bias_prev = jnp.stack([prev, jnp.full_like(prev, NEG)])
    bias_cur = table(qi - kk)
    dist_s = WINDOW - np.arange(WINDOW)
    col = jnp.where(((dist_s >= 0) & (dist_s < WINDOW))[:, None], rel_bias[_bucket(dist_s)], NEG)
    pairs = [jnp.concatenate([jnp.broadcast_to(col[:, r:r + 1], (WINDOW, HEAD_DIM)),
                              jnp.broadcast_to(col[:, GQ + r:GQ + r + 1], (WINDOW, HEAD_DIM))], axis=1)
             for r in range(GQ)]
    new = rel_bias[0]
    bias_new = jnp.stack([_pair_row(new, r) for r in range(GQ)])
    return bias_prev, bias_cur, jnp.stack(pairs), bias_new


def _pair_row(per_head, r):
    return jnp.concatenate([jnp.broadcast_to(per_head[r], (1, HEAD_DIM)),
                            jnp.broadcast_to(per_head[GQ + r], (1, HEAD_DIM))], axis=1)


def _prep_layer(l, tables, w_in, w_out, ln_v_g, ln_v_b, sgu_w, sgu_b, attn_sinks, conv_w, merge_g,
                ln1_g, ln1_b, ln2_g, ln2_b, router_w, router_bias, sw_gate, sw_up, sw_down):
    q0 = 2 * W_A
    qcols = np.concatenate([q0 + h * HEAD_DIM + np.arange(HEAD_DIM) for h in _HEAD_AT])
    cols = np.concatenate([np.arange(q0), qcols, np.arange(q0 + W_B, IN_COLS)])
    rows = np.concatenate([np.arange(W_A), qcols - q0 + W_A, np.arange(W_A + W_B, D_MODEL)])
    row = lambda a: a.reshape(1, -1)
    prev = max(l - 1, 0)
    bias_prev, bias_cur, bias_samp, bias_new = tables
    return dict(
        sinks=attn_sinks[l],
        ln_prev_g=row(ln2_g[prev]), ln_prev_b=row(ln2_b[prev]),
        w_in=w_in[l][:, cols].astype(BF), w_out=w_out[l][rows].astype(BF),
        sgu_w=jnp.tril(sgu_w[l]).reshape(H_A * CHUNK, CHUNK).astype(BF),
        sgu_b=jnp.repeat(sgu_b[l].T, HD_A, axis=1),
        sgu_w0=row(jnp.repeat(sgu_w[l][:, 0, 0].astype(BF).astype(F32), HD_A)),
        sgu_b0=row(jnp.repeat(sgu_b[l][:, 0], HD_A)),
        ln_v_g=row(ln_v_g[l]), ln_v_b=row(ln_v_b[l]),
        bias_prev=bias_prev, bias_cur=bias_cur, bias_samp=bias_samp, bias_new=bias_new,
        sink_row=jnp.stack([_pair_row(attn_sinks[l], r) for r in range(GQ)]),
        conv_w=conv_w[l].T, merge_g=row(merge_g[l][rows]), ln1_g=row(ln1_g[l]), ln1_b=row(ln1_b[l]),
        router_wt=router_w[l].T.astype(BF), router_b=router_bias[l].reshape(N_EXPERTS, 1),
        sw_gu=jnp.concatenate([sw_gate[l], sw_up[l]], axis=1).astype(BF), sw_d=sw_down[l].astype(BF),
    )


def kernel(x_prompt, x_sample, cache_k, cache_v, state_conv, rel_bias, w_in, w_out, ln_v_g, ln_v_b,
           sgu_w, sgu_b, attn_sinks, conv_w, merge_g, ln1_g, ln1_b, ln2_g, ln2_b, router_w, router_bias,
           ew_gate, ew_up, ew_down, sw_gate, sw_up, sw_down):
    tables = _bias_tables(rel_bias)
    xin_p = x_prompt.reshape(T_PROMPT, D_MODEL)
    xin_s = x_sample.reshape(DEC_BATCH, D_MODEL)
    kv_shape = (WINDOW, KV_HEADS, HEAD_DIM)
    outs = [[] for _ in range(8)]
    for l in range(DEPTH):
        p = _prep_layer(l, tables, w_in, w_out, ln_v_g, ln_v_b, sgu_w, sgu_b, attn_sinks, conv_w, merge_g,
                        ln1_g, ln1_b, ln2_g, ln2_b, router_w, router_bias, sw_gate, sw_up, sw_down)
        if l == 0:
            donors = (jnp.zeros((T_ALL * ROWS_F32, LANES), F32), jnp.zeros((T_ALL * ROWS_PK, LANES), jnp.uint32),
                      jnp.zeros((N_EXPERTS, T_ALL), F32))
        else:
            donors = (xp, gw)
        base, xp, gw, klast, vlast, ctail, vrows = _prompt_mixer(l == 0, xin_p, p, donors)
        ck = cache_k[l].astype(BF).reshape(DEC_BATCH, WINDOW, LANES)
        cv = cache_v[l].astype(BF).reshape(DEC_BATCH, WINDOW, LANES)
        s0, s1 = state_conv[l][:, 0], state_conv[l][:, 1]
        base, xp, gw, knew, vnew, hc, vns = _sample_mixer(l == 0, xin_s, base, xp, gw, p, ck, cv, s0, s1)
        off, tok, gate = _dispatch_lists(gw)
        s2 = _experts(off, tok, gate, xp, ew_gate[l], ew_up[l], ew_down[l], base)
        xin_p = xin_s = s2
        outs[0].append(klast.reshape((BATCH,) + kv_shape))
        outs[1].append(vlast.reshape((BATCH,) + kv_shape))
        outs[2].append(ctail)
        outs[3].append(vrows)
        outs[4].append(jnp.concatenate([cache_k[l][:, 1:], knew.reshape((DEC_BATCH, 1) + kv_shape[1:])], axis=1))
        outs[5].append(jnp.concatenate([cache_v[l][:, 1:], vnew.reshape((DEC_BATCH, 1) + kv_shape[1:])], axis=1))
        outs[6].append(jnp.stack([s1, hc], axis=1))
        outs[7].append(vns[:, None])
    g2, b2 = ln2_g[DEPTH - 1].reshape(1, -1), ln2_b[DEPTH - 1].reshape(1, -1)
    y_prompt = _final_ln(s2, g2, b2, TM, T_PROMPT // TM, 0).reshape(BATCH, SEQ, D_MODEL)
    y_sample = _final_ln(s2, g2, b2, DEC_BATCH, 1, T_PROMPT // DEC_BATCH).reshape(DEC_BATCH, 1, D_MODEL)
    return (y_prompt, y_sample) + tuple(jnp.stack(o) for o in outs)
```

```python
import functools
import math

import numpy as np
import jax
import jax.numpy as jnp
from jax import lax
from jax.experimental import pallas as pl
from jax.experimental.pallas import tpu as pltpu

D_MODEL = 1024
BATCH = 2
SEQ = 8192
DEPTH = 2
DEC_BATCH = 128
PAST_LEN = 16384
W_A = 256
H_A = 4
HD_A = 64
CHUNK = 128
W_B = 512
HEAD_DIM = 64
N_HEADS = 8
KV_HEADS = 2
GQ = 4
WINDOW = 128
W_C = 256
CONV_W = 3
N_BUCKETS = 32
MAX_EXACT = 16
MAX_DIST = 128
N_EXPERTS = 64
TOP_K = 8
N_GROUPS = 8
TOPK_GROUPS = 4
GROUP_SIZE = N_EXPERTS // N_GROUPS
D_EXPERT = 256
ROUTE_SCALE = 2.5
ALPHA = (2 * DEPTH) ** 0.25
LN_EPS = 1e-5
IN_COLS = 2048

T_PROMPT = BATCH * SEQ
T_ALL = T_PROMPT + DEC_BATCH

LANES = 128
SUBLANES = 8
VMEM_LIMIT = 56 * 1024 * 1024

TM = 512
NBLK = TM // WINDOW
TILES_PER_SEQ = SEQ // TM
ROWS_F32 = D_MODEL // LANES
ROWS_PK = ROWS_F32 // 2
N_TGROUPS = 4
TG = T_ALL // N_TGROUPS
RB = 128
NEG = -1e30

BF = jnp.bfloat16
F32 = jnp.float32


def _ln(x, g, b):
    mu = jnp.mean(x, -1, keepdims=True)
    xc = x - mu
    var = jnp.mean(xc * xc, -1, keepdims=True)
    return xc * lax.rsqrt(var + LN_EPS) * g + b


def _rms(x, g):
    return x * lax.rsqrt(jnp.mean(x * x, -1, keepdims=True) + LN_EPS) * g


def _dot(a, b):
    return jnp.dot(a, b, preferred_element_type=F32)


def _dot_nt(a, b):
    return lax.dot_general(a, b, (((1,), (1,)), ((), ())), preferred_element_type=F32)


def _pack_pairs(lo, hi):
    lo_u = pltpu.bitcast(lo.astype(BF).astype(F32), jnp.uint32)
    hi_u = pltpu.bitcast(hi.astype(BF).astype(F32), jnp.uint32)
    return (hi_u & jnp.uint32(0xFFFF0000)) | (lo_u >> 16)


def _route(logits_t, rbias):
    n = logits_t.shape[1]
    scores = jax.nn.sigmoid(logits_t)
    biased = scores + rbias
    g3 = biased.reshape(N_GROUPS, GROUP_SIZE, n)
    sub = lax.broadcasted_iota(jnp.int32, g3.shape, 1)
    m1 = jnp.max(g3, axis=1, keepdims=True)
    first = jnp.min(jnp.where(g3 == m1, sub, GROUP_SIZE), axis=1, keepdims=True)
    m2 = jnp.max(jnp.where(sub == first, -jnp.inf, g3), axis=1, keepdims=True)
    gs = (m1 + m2).reshape(N_GROUPS, n)
    gi = lax.broadcasted_iota(jnp.int32, gs.shape, 0)
    grank = jnp.zeros(gs.shape, jnp.int32)
    for j in range(N_GROUPS):
        row = gs[j:j + 1]
        grank = grank + ((row > gs) | ((row == gs) & (j < gi))).astype(jnp.int32)
    gsel = (grank < TOPK_GROUPS).reshape(N_GROUPS, 1, n)
    masked = jnp.where(gsel, g3, -jnp.inf).reshape(N_EXPERTS, n)
    ei = lax.broadcasted_iota(jnp.int32, masked.shape, 0)
    rank = jnp.zeros(masked.shape, jnp.int32)
    for j in range(N_EXPERTS):
        row = masked[j:j + 1]
        rank = rank + ((row > masked) | ((row == masked) & (j < ei))).astype(jnp.int32)
    sel = rank < TOP_K
    w = jnp.where(sel, scores, 0.0)
    return w / jnp.sum(w, axis=0, keepdims=True) * ROUTE_SCALE


def _post_mixers(x, ya, yb, yc, mg_ref, wout_ref, ln1g_ref, ln1b_ref, wrt_ref, rb_ref,
                 swgu_ref, swd_ref, base_ref, xp_ref, gw_ref):
    n = x.shape[0]
    mg = mg_ref[...]
    cat = jnp.concatenate([_rms(ya, mg[:, :W_A]), _rms(yb, mg[:, W_A:W_A + W_B]),
                           _rms(yc, mg[:, W_A + W_B:])], axis=-1)
    m = _dot(cat.astype(BF), wout_ref[...])
    x1 = _ln(ALPHA * x + m, ln1g_ref[...], ln1b_ref[...])
    x1b = x1.astype(BF)
    gw_ref[...] = _route(_dot_nt(wrt_ref[...], x1b), rb_ref[...])
    hg = _dot(x1b, swgu_ref[...])
    hs = jax.nn.silu(hg[:, :D_EXPERT]) * hg[:, D_EXPERT:]
    base = ALPHA * x1 + _dot(hs.astype(BF), swd_ref[...])
    for c in range(ROWS_F32):
        base_ref[pl.ds(c, n, stride=ROWS_F32), :] = base[:, c * LANES:(c + 1) * LANES]
    half = D_MODEL // 2
    for j in range(ROWS_PK):
        xp_ref[pl.ds(j, n, stride=ROWS_PK), :] = _pack_pairs(
            x1[:, j * LANES:(j + 1) * LANES], x1[:, half + j * LANES:half + (j + 1) * LANES])


def _load_tokens(first_layer, xin_ref, n, lng_ref, lnb_ref):
    if first_layer:
        return xin_ref[...]
    s = jnp.concatenate([xin_ref[pl.ds(c, n, stride=ROWS_F32), :] for c in range(ROWS_F32)], axis=-1)
    return _ln(s, lng_ref[...], lnb_ref[...])


def _head_mask_sum(r):
    lane = lax.broadcasted_iota(jnp.int32, (CHUNK, W_A), 1) // HD_A
    out = r[0:CHUNK]
    for h in range(1, H_A):
        out = jnp.where(lane == h, r[h * CHUNK:(h + 1) * CHUNK], out)
    return out


def _prompt_mixer_kernel(first_layer, sinks_ref, *refs):
    (xin_ref, lng_ref, lnb_ref, win_ref, wout_ref, sguw_ref,
     sgub_ref, lnvg_ref, lnvb_ref, bprev_ref, bcur_ref, convw_ref, mg_ref, ln1g_ref,
     ln1b_ref, wrt_ref, rb_ref, swgu_ref, swd_ref,
     base_ref, xp_ref, gw_ref, klast_ref, vlast_ref, ctail_ref, vrows_ref,
     kprev_s, vprev_s, hcprev_s) = refs[(3 if first_layer else 2):]
    t = pl.program_id(1)
    is_first = t == 0

    @pl.when(is_first)
    def _():
        kprev_s[...] = jnp.zeros_like(kprev_s)
        vprev_s[...] = jnp.zeros_like(vprev_s)
        hcprev_s[...] = jnp.zeros_like(hcprev_s)

    x = _load_tokens(first_layer, xin_ref, TM, lng_ref, lnb_ref)
    z = _dot(x.astype(BF), win_ref[...])
    ua = jax.nn.gelu(z[:, 0:W_A])
    vn = _ln(jax.nn.gelu(z[:, W_A:2 * W_A]), lnvg_ref[...], lnvb_ref[...])
    q = z[:, 512:1024]
    k = z[:, 1024:1152]
    v = z[:, 1152:1280]
    gate_b = z[:, 1280:1536]
    gate_c = z[:, 1536:1792]
    h = z[:, 1792:2048]

    vnb = vn.astype(BF)
    sgub = sgub_ref[...]
    mixed = [_head_mask_sum(_dot(sguw_ref[...], vnb[c * CHUNK:(c + 1) * CHUNK])) + sgub for c in range(NBLK)]
    ya = ua * jnp.concatenate(mixed, axis=0)

    kb = k.astype(BF)
    vb = v.astype(BF)
    qb = q.astype(BF)
    scale = HEAD_DIM ** -0.5
    yb_blocks = []
    for c in range(NBLK):
        if c == 0:
            kp, vp = kprev_s[...], vprev_s[...]
            bsel = jnp.where(is_first, 1, 0)
        else:
            kp, vp = kb[(c - 1) * WINDOW:c * WINDOW], vb[(c - 1) * WINDOW:c * WINDOW]
            bsel = 0
        kc, vc = kb[c * WINDOW:(c + 1) * WINDOW], vb[c * WINDOW:(c + 1) * WINDOW]
        qc = qb[c * WINDOW:(c + 1) * WINDOW]
        outs = [None] * N_HEADS
        for g in range(KV_HEADS):
            gs = slice(g * HEAD_DIM, (g + 1) * HEAD_DIM)
            qg = jnp.concatenate([qc[:, (2 * r + g) * HEAD_DIM:(2 * r + g + 1) * HEAD_DIM] for r in range(GQ)], axis=0)
            sp = _dot_nt(qg, kp[:, gs]) * scale + bprev_ref[bsel, g]
            sc = _dot_nt(qg, kc[:, gs]) * scale + bcur_ref[g]
            for r in range(GQ):
                rs = slice(r * WINDOW, (r + 1) * WINDOW)
                sink = sinks_ref[g * GQ + r]
                spr, scr = sp[rs], sc[rs]
                mx = jnp.maximum(jnp.maximum(jnp.max(spr, -1, keepdims=True), jnp.max(scr, -1, keepdims=True)), sink)
                ep, ec = jnp.exp(spr - mx), jnp.exp(scr - mx)
                den = jnp.sum(ep, -1, keepdims=True) + jnp.sum(ec, -1, keepdims=True) + jnp.exp(sink - mx)
                o = _dot((ep / den).astype(BF), vp[:, gs]) + _dot((ec / den).astype(BF), vc[:, gs])
                outs[2 * r + g] = o
        yb_blocks.append(jnp.concatenate(outs, axis=-1))
    yb = jnp.concatenate(yb_blocks, axis=0)
    kprev_s[...] = kb[TM - WINDOW:]
    vprev_s[...] = vb[TM - WINDOW:]

    hc = gate_c * h
    prev = hcprev_s[...]
    row = lax.broadcasted_iota(jnp.int32, hc.shape, 0)
    h1 = jnp.where(row == 0, prev[7:8], pltpu.roll(hc, 1, 0))
    h2 = jnp.where(row == 0, prev[6:7], jnp.where(row == 1, prev[7:8], pltpu.roll(hc, 2, 0)))
    cw = convw_ref[...]
    yc = gate_b * (h2 * cw[0:1] + h1 * cw[1:2] + hc * cw[2:3])
    hcprev_s[...] = hc[TM - SUBLANES:]

    @pl.when(t == TILES_PER_SEQ - 1)
    def _():
        klast_ref[0] = k[TM - WINDOW:]
        vlast_ref[0] = v[TM - WINDOW:]
        ctail_ref[0] = hc[TM - (CONV_W - 1):]
        vrows_ref[0] = vn[TM - CHUNK:]

    _post_mixers(x, ya, yb, yc, mg_ref, wout_ref, ln1g_ref, ln1b_ref, wrt_ref, rb_ref, swgu_ref, swd_ref,
                 base_ref, xp_ref, gw_ref)


def _full(shape):
    nd = len(shape)
    return pl.BlockSpec(shape, lambda *_: (0,) * nd)


def _prompt_mixer(first_layer, xin, p, donors):
    nt = TILES_PER_SEQ
    tok = lambda b, t, *_: (b * nt + t, 0)
    anyspec = pl.BlockSpec(memory_space=pl.ANY)
    if first_layer:
        xin_spec = pl.BlockSpec((TM, D_MODEL), tok)
        aliases = {1: 0, 2: 1, 3: 2}
    else:
        xin_spec = pl.BlockSpec((TM * ROWS_F32, LANES), tok)
        aliases = {3: 0, 1: 1, 2: 2}
    in_specs = [anyspec] * len(donors) + [
        xin_spec, _full((1, D_MODEL)), _full((1, D_MODEL)),
        _full((D_MODEL, IN_COLS)), _full((D_MODEL, D_MODEL)), _full((H_A * CHUNK, CHUNK)),
        _full((CHUNK, W_A)), _full((1, W_A)), _full((1, W_A)),
        _full((2, KV_HEADS, GQ * WINDOW, WINDOW)), _full((KV_HEADS, GQ * WINDOW, WINDOW)),
        _full((CONV_W, W_C)), _full((1, D_MODEL)), _full((1, D_MODEL)), _full((1, D_MODEL)),
        _full((N_EXPERTS, D_MODEL)), _full((N_EXPERTS, 1)), _full((D_MODEL, 2 * D_EXPERT)),
        _full((D_EXPERT, D_MODEL)),
    ]
    per_seq = lambda b, t, *_: (b, 0, 0)
    out_shape = [
        jax.ShapeDtypeStruct((T_ALL * ROWS_F32, LANES), F32),
        jax.ShapeDtypeStruct((T_ALL * ROWS_PK, LANES), jnp.uint32),
        jax.ShapeDtypeStruct((N_EXPERTS, T_ALL), F32),
        jax.ShapeDtypeStruct((BATCH, WINDOW, KV_HEADS * HEAD_DIM), F32),
        jax.ShapeDtypeStruct((BATCH, WINDOW, KV_HEADS * HEAD_DIM), F32),
        jax.ShapeDtypeStruct((BATCH, CONV_W - 1, W_C), F32),
        jax.ShapeDtypeStruct((BATCH, CHUNK, W_A), F32),
    ]
    out_specs = [
        pl.BlockSpec((TM * ROWS_F32, LANES), tok),
        pl.BlockSpec((TM * ROWS_PK, LANES), tok),
        pl.BlockSpec((N_EXPERTS, TM), lambda b, t, *_: (0, b * nt + t)),
        pl.BlockSpec((1, WINDOW, KV_HEADS * HEAD_DIM), per_seq),
        pl.BlockSpec((1, WINDOW, KV_HEADS * HEAD_DIM), per_seq),
        pl.BlockSpec((1, CONV_W - 1, W_C), per_seq),
        pl.BlockSpec((1, CHUNK, W_A), per_seq),
    ]
    return pl.pallas_call(
        functools.partial(_prompt_mixer_kernel, first_layer),
        grid_spec=pltpu.PrefetchScalarGridSpec(
            num_scalar_prefetch=1, grid=(BATCH, nt), in_specs=in_specs, out_specs=out_specs,
            scratch_shapes=[pltpu.VMEM((WINDOW, KV_HEADS * HEAD_DIM), BF),
                            pltpu.VMEM((WINDOW, KV_HEADS * HEAD_DIM), BF),
                            pltpu.VMEM((SUBLANES, W_C), F32)]),
        out_shape=out_shape,
        input_output_aliases=aliases,
        compiler_params=pltpu.CompilerParams(dimension_semantics=("arbitrary", "arbitrary"),
                                             vmem_limit_bytes=VMEM_LIMIT),
        name="prompt_mixer",
    )(p["sinks"], *donors, xin, p["ln_prev_g"], p["ln_prev_b"], p["w_in"], p["w_out"], p["sgu_w"], p["sgu_b"],
      p["ln_v_g"], p["ln_v_b"], p["bias_prev"], p["bias_cur"], p["conv_w"], p["merge_g"], p["ln1_g"],
      p["ln1_b"], p["router_wt"], p["router_b"], p["sw_gu"], p["sw_d"])


def _half_sums(a, lane_lo):
    s0 = jnp.sum(jnp.where(lane_lo, a, 0.0), axis=1, keepdims=True)
    s1 = jnp.sum(jnp.where(lane_lo, 0.0, a), axis=1, keepdims=True)
    return jnp.where(lane_lo, s0, s1)


def _sample_mixer_kernel(first_layer, *refs):
    (xin_ref, lng_ref, lnb_ref, win_ref, wout_ref,
     sguw0_ref, sgub0_ref, lnvg_ref, lnvb_ref, ck_ref, cv_ref, bsamp_ref, bnew_ref,
     sinkrow_ref, s0_ref, s1_ref, convw_ref, mg_ref, ln1g_ref, ln1b_ref, wrt_ref,
     rb_ref, swgu_ref, swd_ref,
     base_ref, xp_ref, gw_ref, knew_ref, vnew_ref, hc_ref, vn_ref,
     q_s, yb_s, kn_s, vn_s) = refs[(3 if first_layer else 2):]
    n = DEC_BATCH
    x = _load_tokens(first_layer, xin_ref, n, lng_ref, lnb_ref)
    z = _dot(x.astype(BF), win_ref[...])
    ua = jax.nn.gelu(z[:, 0:W_A])
    vn = _ln(jax.nn.gelu(z[:, W_A:2 * W_A]), lnvg_ref[...], lnvb_ref[...])
    vn_ref[...] = vn
    ya = ua * (vn.astype(BF).astype(F32) * sguw0_ref[...] + sgub0_ref[...])
    k = z[:, 1024:1152]
    v = z[:, 1152:1280]
    knew_ref[...] = k
    vnew_ref[...] = v
    q_s[...] = z[:, 512:1024].astype(BF).astype(F32)
    kn_s[...] = k.astype(BF).astype(F32)
    vn_s[...] = v.astype(BF).astype(F32)
    scale = HEAD_DIM ** -0.5
    lane_lo = lax.broadcasted_iota(jnp.int32, (1, LANES), 1) < HEAD_DIM

    def per_seq_tile(bi, carry):
        b0 = pl.multiple_of(bi * SUBLANES, SUBLANES)
        q8 = q_s[pl.ds(b0, SUBLANES), :]
        kn8 = kn_s[pl.ds(b0, SUBLANES), :]
        vn8 = vn_s[pl.ds(b0, SUBLANES), :]
        rows = []
        for u in range(SUBLANES):
            kb = ck_ref[b0 + u].astype(F32)
            vb = cv_ref[b0 + u].astype(F32)
            kn = kn8[u:u + 1]
            vnw = vn8[u:u + 1]
            pieces = []
            for r in range(GQ):
                qp = q8[u:u + 1, r * LANES:(r + 1) * LANES]
                s = _half_sums(kb * qp, lane_lo) * scale + bsamp_ref[r]
                sn = _half_sums(kn * qp, lane_lo) * scale + bnew_ref[r]
                sink = sinkrow_ref[r]
                mx = jnp.maximum(jnp.maximum(jnp.max(s, axis=0, keepdims=True), sn), sink)
                e = jnp.exp(s - mx)
                en = jnp.exp(sn - mx)
                den = jnp.sum(e, axis=0, keepdims=True) + en + jnp.exp(sink - mx)
                p = (e / den).astype(BF).astype(F32)
                pn = (en / den).astype(BF).astype(F32)
                pieces.append(jnp.sum(p * vb, axis=0, keepdims=True) + pn * vnw)
            rows.append(jnp.concatenate(pieces, axis=1))
        yb_s[pl.ds(b0, SUBLANES), :] = jnp.concatenate(rows, axis=0)
        return carry

    lax.fori_loop(0, n // SUBLANES, per_seq_tile, 0)
    yb = yb_s[...]

    hc = z[:, 1536:1792] * z[:, 1792:2048]
    hc_ref[...] = hc
    cw = convw_ref[...]
    yc = z[:, 1280:1536] * (s0_ref[...] * cw[0:1] + s1_ref[...] * cw[1:2] + hc * cw[2:3])
    _post_mixers(x, ya, yb, yc, mg_ref, wout_ref, ln1g_ref, ln1b_ref, wrt_ref, rb_ref, swgu_ref, swd_ref,
                 base_ref, xp_ref, gw_ref)


def _sample_mixer(first_layer, xin, base, xp, gw, p, ck, cv, s0, s1):
    n = DEC_BATCH
    tail = T_PROMPT // n
    anyspec = pl.BlockSpec(memory_space=pl.ANY)
    if first_layer:
        xin_spec = pl.BlockSpec((n, D_MODEL), lambda i: (0, 0))
        lead, lead_specs, aliases = (base, xp, gw, xin), [anyspec] * 3, {0: 0, 1: 1, 2: 2}
    else:
        xin_spec = pl.BlockSpec((n * ROWS_F32, LANES), lambda i: (tail, 0))
        lead, lead_specs, aliases = (xp, gw, base), [anyspec] * 2, {2: 0, 0: 1, 1: 2}
    in_specs = lead_specs + [
        xin_spec, _full((1, D_MODEL)), _full((1, D_MODEL)),
        _full((D_MODEL, IN_COLS)), _full((D_MODEL, D_MODEL)), _full((1, W_A)), _full((1, W_A)),
        _full((1, W_A)), _full((1, W_A)),
        _full((n, WINDOW, LANES)), _full((n, WINDOW, LANES)),
        _full((GQ, WINDOW, LANES)), _full((GQ, 1, LANES)), _full((GQ, 1, LANES)),
        _full((n, W_C)), _full((n, W_C)),
        _full((CONV_W, W_C)), _full((1, D_MODEL)), _full((1, D_MODEL)), _full((1, D_MODEL)),
        _full((N_EXPERTS, D_MODEL)), _full((N_EXPERTS, 1)), _full((D_MODEL, 2 * D_EXPERT)),
        _full((D_EXPERT, D_MODEL)),
    ]
    out_shape = [
        jax.ShapeDtypeStruct(base.shape, base.dtype),
        jax.ShapeDtypeStruct(xp.shape, xp.dtype),
        jax.ShapeDtypeStruct(gw.shape, gw.dtype),
        jax.ShapeDtypeStruct((n, LANES), F32),
        jax.ShapeDtypeStruct((n, LANES), F32),
        jax.ShapeDtypeStruct((n, W_C), F32),
        jax.ShapeDtypeStruct((n, W_A), F32),
    ]
    out_specs = [
        pl.BlockSpec((n * ROWS_F32, LANES), lambda i: (tail, 0)),
        pl.BlockSpec((n * ROWS_PK, LANES), lambda i: (tail, 0)),
        pl.BlockSpec((N_EXPERTS, n), lambda i: (0, tail)),
        _full((n, LANES)), _full((n, LANES)), _full((n, W_C)), _full((n, W_A)),
    ]
    return pl.pallas_call(
        functools.partial(_sample_mixer_kernel, first_layer),
        grid=(1,), in_specs=in_specs, out_specs=out_specs, out_shape=out_shape,
        scratch_shapes=[pltpu.VMEM((n, W_B), F32), pltpu.VMEM((n, W_B), F32),
                        pltpu.VMEM((n, LANES), F32), pltpu.VMEM((n, LANES), F32)],
        input_output_aliases=aliases,
        compiler_params=pltpu.CompilerParams(dimension_semantics=("arbitrary",),
                                             vmem_limit_bytes=VMEM_LIMIT),
        name="sample_mixer",
    )(*lead, p["ln_prev_g"], p["ln_prev_b"], p["w_in"], p["w_out"], p["sgu_w0"], p["sgu_b0"],
      p["ln_v_g"], p["ln_v_b"], ck, cv, p["bias_samp"], p["bias_new"], p["sink_row"], s0, s1,
      p["conv_w"], p["merge_g"], p["ln1_g"], p["ln1_b"], p["router_wt"], p["router_b"], p["sw_gu"],
      p["sw_d"])


MAX_BLOCKS = (TG * TOP_K + N_EXPERTS * (RB - 1)) // RB
LIST_LEN = -(-((MAX_BLOCKS + 2) * RB) // 1024) * 1024
RMW_BATCH = 8


def _expert_kernel(bstart_ref, tok_ref, gate_ref, wg_ref, wu_ref, wdn_ref, xp_any, base_any, out_any,
                   acc, xpv, xs0, xs1, xbf, yb0, yb1, wgu_s, wd_s, sem_acc, sem_xp):
    g = pl.program_id(0)
    e = pl.program_id(1)
    rows = TG * ROWS_F32
    rows_pk = TG * ROWS_PK

    def acc_copy(to_vmem):
        hbm = (base_any if to_vmem else out_any).at[pl.ds(g * rows, rows)]
        vm = acc.at[pl.ds(0, rows)]
        return pltpu.make_async_copy(hbm, vm, sem_acc) if to_vmem else pltpu.make_async_copy(vm, hbm, sem_acc)

    def gather(b, xs_dst):
        base = b * RB
        for r in range(RB):
            src = pl.multiple_of(tok_ref[base + r] * ROWS_PK, ROWS_PK)
            xs_dst[r * ROWS_PK:(r + 1) * ROWS_PK, :] = xpv[pl.ds(src, ROWS_PK), :]

    def scatter(b, yb_src):
        base = b * RB
        for i in range(RB // RMW_BATCH):
            dsts, vals = [], []
            for u in range(RMW_BATCH):
                r = i * RMW_BATCH + u
                dst = pl.multiple_of(tok_ref[base + r] * ROWS_F32, ROWS_F32)
                dsts.append(dst)
                vals.append(acc[pl.ds(dst, ROWS_F32), :]
                            + gate_ref[base + r] * yb_src[r * ROWS_F32:(r + 1) * ROWS_F32, :])
            for dst, val in zip(dsts, vals):
                acc[pl.ds(dst, ROWS_F32), :] = val

    @pl.when(e == 0)
    def _():
        cp_acc = acc_copy(True)
        cp_xp = pltpu.make_async_copy(xp_any.at[pl.ds(g * rows_pk, rows_pk)], xpv.at[pl.ds(0, rows_pk)], sem_xp)
        cp_acc.start()
        cp_xp.start()
        acc[rows:rows + ROWS_F32, :] = jnp.zeros((ROWS_F32, LANES), F32)
        xpv[rows_pk:rows_pk + ROWS_PK, :] = jnp.zeros((ROWS_PK, LANES), jnp.uint32)
        yb0[...] = jnp.zeros_like(yb0)
        yb1[...] = jnp.zeros_like(yb1)
        cp_xp.wait()
        gather(1, xs1)
        cp_acc.wait()

    wgu_s[:, :D_EXPERT] = wg_ref[0, 0].astype(BF)
    wgu_s[:, D_EXPERT:] = wu_ref[0, 0].astype(BF)
    wd_s[...] = wdn_ref[0, 0].astype(BF)

    first = bstart_ref[g * (N_EXPERTS + 1) + e]
    last = bstart_ref[g * (N_EXPERTS + 1) + e + 1]

    def step(b, xs_cur, xs_nxt, yb_cur, yb_prev):
        gather(b + 1, xs_nxt)
        half = D_MODEL // 2
        for j in range(ROWS_PK):
            u32 = xs_cur[pl.ds(j, RB, stride=ROWS_PK), :]
            xbf[:, j * LANES:(j + 1) * LANES] = pltpu.bitcast(u32 << 16, F32).astype(BF)
            xbf[:, half + j * LANES:half + (j + 1) * LANES] = pltpu.bitcast(
                u32 & jnp.uint32(0xFFFF0000), F32).astype(BF)
        hg = _dot(xbf[...], wgu_s[...])
        hid = jax.nn.silu(hg[:, :D_EXPERT]) * hg[:, D_EXPERT:]
        y = _dot(hid.astype(BF), wd_s[...])
        scatter(b - 1, yb_prev)
        for c in range(ROWS_F32):
            yb_cur[pl.ds(c, RB, stride=ROWS_F32), :] = y[:, c * LANES:(c + 1) * LANES]

    def block(b, carry):
        lax.cond((b & 1) == 0,
                 lambda: step(b, xs0, xs1, yb0, yb1),
                 lambda: step(b, xs1, xs0, yb1, yb0))
        return carry

    lax.fori_loop(first, last, block, 0)

    @pl.when(e == N_EXPERTS - 1)
    def _():
        lax.cond(((last - 1) & 1) == 0, lambda: scatter(last - 1, yb0), lambda: scatter(last - 1, yb1))
        cp = acc_copy(False)
        cp.start()
        cp.wait()


def _experts(layer, bstart, tok, gate, xp, wg, wu, wdn, base):
    lists = pl.BlockSpec((LIST_LEN,), lambda g, e, *_: (g,), memory_space=pltpu.SMEM)
    anyspec = pl.BlockSpec(memory_space=pl.ANY)
    in_specs = [
        lists, lists,
        pl.BlockSpec((1, 1, D_MODEL, D_EXPERT), lambda g, e, *_: (layer, e, 0, 0)),
        pl.BlockSpec((1, 1, D_MODEL, D_EXPERT), lambda g, e, *_: (layer, e, 0, 0)),
        pl.BlockSpec((1, 1, D_EXPERT, D_MODEL), lambda g, e, *_: (layer, e, 0, 0)),
        anyspec, anyspec,
    ]
    return pl.pallas_call(
        _expert_kernel,
        grid_spec=pltpu.PrefetchScalarGridSpec(
            num_scalar_prefetch=1, grid=(N_TGROUPS, N_EXPERTS), in_specs=in_specs,
            out_specs=anyspec,
            scratch_shapes=[pltpu.VMEM(((TG + 1) * ROWS_F32, LANES), F32),
                            pltpu.VMEM(((TG + 1) * ROWS_PK, LANES), jnp.uint32),
                            pltpu.VMEM((RB * ROWS_PK, LANES), jnp.uint32),
                            pltpu.VMEM((RB * ROWS_PK, LANES), jnp.uint32),
                            pltpu.VMEM((RB, D_MODEL), BF),
                            pltpu.VMEM((RB * ROWS_F32, LANES), F32),
                            pltpu.VMEM((RB * ROWS_F32, LANES), F32),
                            pltpu.VMEM((D_MODEL, 2 * D_EXPERT), BF),
                            pltpu.VMEM((D_EXPERT, D_MODEL), BF),
                            pltpu.SemaphoreType.DMA, pltpu.SemaphoreType.DMA]),
        out_shape=jax.ShapeDtypeStruct(base.shape, base.dtype),
        input_output_aliases={7: 0},
        compiler_params=pltpu.CompilerParams(dimension_semantics=("arbitrary", "arbitrary"),
                                             vmem_limit_bytes=VMEM_LIMIT),
        name="experts",
    )(bstart, tok, gate, wg, wu, wdn, xp, base)


def _dispatch_lists(gw):
    per_group = lambda a: a.reshape(N_EXPERTS, N_TGROUPS, TG).transpose(1, 0, 2)
    chosen = per_group(gw > 0)
    counts = jnp.sum(chosen, axis=-1, dtype=jnp.int32)
    zero = jnp.zeros((N_TGROUPS, 1), jnp.int32)
    off = jnp.concatenate([zero, jnp.cumsum(counts, axis=1)], axis=1)
    bstart = 1 + jnp.concatenate([zero, jnp.cumsum((counts + RB - 1) // RB, axis=1)], axis=1)
    flat = chosen.reshape(N_TGROUPS, N_EXPERTS * TG)
    idx = jax.vmap(lambda f: jnp.nonzero(f, size=TG * TOP_K, fill_value=0)[0])(flat).astype(jnp.int32)
    gate = jnp.take_along_axis(per_group(gw).reshape(N_TGROUPS, N_EXPERTS * TG), idx, axis=1)
    pos = jnp.arange(LIST_LEN, dtype=jnp.int32)[None, :]
    blk = pos // RB
    owner = jnp.sum(blk[:, :, None] >= bstart[:, None, 1:], axis=-1, dtype=jnp.int32)
    owner = jnp.minimum(owner, N_EXPERTS - 1)
    within = pos - jnp.take_along_axis(bstart, owner, axis=1) * RB
    valid = (within >= 0) & (within < jnp.take_along_axis(counts, owner, axis=1))
    src = jnp.clip(jnp.take_along_axis(off, owner, axis=1) + within, 0, TG * TOP_K - 1)
    pair = jnp.stack([idx % TG, lax.bitcast_convert_type(gate, jnp.int32)], axis=-1)
    picked = jnp.take_along_axis(pair, src[:, :, None], axis=1)
    tok = jnp.where(valid, picked[..., 0], TG)
    gate = jnp.where(valid, lax.bitcast_convert_type(picked[..., 1], F32), 0.0)
    return bstart.reshape(-1), tok.reshape(-1), gate.reshape(-1)


def _final_ln_kernel(n, s_ref, g_ref, b_ref, o_ref):
    o_ref[...] = _load_tokens(False, s_ref, n, g_ref, b_ref)


def _final_ln(s2, g, b, n, steps, first_block):
    return pl.pallas_call(
        functools.partial(_final_ln_kernel, n),
        grid=(steps,),
        in_specs=[pl.BlockSpec((n * ROWS_F32, LANES), lambda i: (first_block + i, 0)),
                  _full((1, D_MODEL)), _full((1, D_MODEL))],
        out_specs=pl.BlockSpec((n, D_MODEL), lambda i: (i, 0)),
        out_shape=jax.ShapeDtypeStruct((n * steps, D_MODEL), F32),
        compiler_params=pltpu.CompilerParams(dimension_semantics=("arbitrary",)),
        name="final_norm",
    )(s2, g, b)


_HEAD_AT = [(p % 2) * GQ + p // 2 for p in range(N_HEADS)]


def _bucket(dist):
    n = np.maximum(dist, 0)
    nf = np.maximum(n, 1).astype(np.float32)
    large = MAX_EXACT + (np.log(nf / np.float32(MAX_EXACT)) / np.float32(math.log(MAX_DIST / MAX_EXACT))
                         * np.float32(N_BUCKETS - MAX_EXACT)).astype(np.int32)
    return np.where(n < MAX_EXACT, n, np.minimum(large, N_BUCKETS - 1)).astype(np.int32)


def _bias_tables(rel_bias):
    qi = np.arange(WINDOW)[:, None]
    kk = np.arange(WINDOW)[None, :]

    def table(dist):
        valid = (dist >= 0) & (dist < WINDOW)
        t = jnp.where(valid[..., None], rel_bias[_bucket(dist)], NEG)
        return jnp.moveaxis(t, -1, 0).reshape(KV_HEADS, GQ * WINDOW, WINDOW)

    prev = table(WINDOW + qi - kk)
    bias_prev = jnp.stack([prev, jnp.full_like(prev, NEG)])
    bias_cur = table(qi - kk)
    dist_s = WINDOW - np.arange(WINDOW)
    col = jnp.where(((dist_s >= 0) & (dist_s < WINDOW))[:, None], rel_bias[_bucket(dist_s)], NEG)
    pairs = [jnp.concatenate([jnp.broadcast_to(col[:, r:r + 1], (WINDOW, HEAD_DIM)),
                              jnp.broadcast_to(col[:, GQ + r:GQ + r + 1], (WINDOW, HEAD_DIM))], axis=1)
             for r in range(GQ)]
    new = rel_bias[0]
    bias_new = jnp.stack([_pair_row(new, r) for r in range(GQ)])
    return bias_prev, bias_cur, jnp.stack(pairs), bias_new


def _pair_row(per_head, r):
    return jnp.concatenate([jnp.broadcast_to(per_head[r], (1, HEAD_DIM)),
                            jnp.broadcast_to(per_head[GQ + r], (1, HEAD_DIM))], axis=1)


def _prep_layer(l, tables, w_in, w_out, ln_v_g, ln_v_b, sgu_w, sgu_b, attn_sinks, conv_w, merge_g,
                ln1_g, ln1_b, ln2_g, ln2_b, router_w, router_bias, sw_gate, sw_up, sw_down):
    q0 = 2 * W_A
    qcols = np.concatenate([q0 + h * HEAD_DIM + np.arange(HEAD_DIM) for h in _HEAD_AT])
    cols = np.concatenate([np.arange(q0), qcols, np.arange(q0 + W_B, IN_COLS)])
    rows = np.concatenate([np.arange(W_A), qcols - q0 + W_A, np.arange(W_A + W_B, D_MODEL)])
    row = lambda a: a.reshape(1, -1)
    prev = max(l - 1, 0)
    bias_prev, bias_cur, bias_samp, bias_new = tables
    return dict(
        sinks=attn_sinks[l],
        ln_prev_g=row(ln2_g[prev]), ln_prev_b=row(ln2_b[prev]),
        w_in=w_in[l][:, cols].astype(BF), w_out=w_out[l][rows].astype(BF),
        sgu_w=jnp.tril(sgu_w[l]).reshape(H_A * CHUNK, CHUNK).astype(BF),
        sgu_b=jnp.repeat(sgu_b[l].T, HD_A, axis=1),
        sgu_w0=row(jnp.repeat(sgu_w[l][:, 0, 0].astype(BF).astype(F32), HD_A)),
        sgu_b0=row(jnp.repeat(sgu_b[l][:, 0], HD_A)),
        ln_v_g=row(ln_v_g[l]), ln_v_b=row(ln_v_b[l]),
        bias_prev=bias_prev, bias_cur=bias_cur, bias_samp=bias_samp, bias_new=bias_new,
        sink_row=jnp.stack([_pair_row(attn_sinks[l], r) for r in range(GQ)]),
        conv_w=conv_w[l].T, merge_g=row(merge_g[l][rows]), ln1_g=row(ln1_g[l]), ln1_b=row(ln1_b[l]),
        router_wt=router_w[l].T.astype(BF), router_b=router_bias[l].reshape(N_EXPERTS, 1),
        sw_gu=jnp.concatenate([sw_gate[l], sw_up[l]], axis=1).astype(BF), sw_d=sw_down[l].astype(BF),
    )


def kernel(x_prompt, x_sample, cache_k, cache_v, state_conv, rel_bias, w_in, w_out, ln_v_g, ln_v_b,
           sgu_w, sgu_b, attn_sinks, conv_w, merge_g, ln1_g, ln1_b, ln2_g, ln2_b, router_w, router_bias,
           ew_gate, ew_up, ew_down, sw_gate, sw_up, sw_down):
    tables = _bias_tables(rel_bias)
    xin_p = x_prompt.reshape(T_PROMPT, D_MODEL)
    xin_s = x_sample.reshape(DEC_BATCH, D_MODEL)
    kv_shape = (WINDOW, KV_HEADS, HEAD_DIM)
    outs = [[] for _ in range(8)]
    for l in range(DEPTH):
        p = _prep_layer(l, tables, w_in, w_out, ln_v_g, ln_v_b, sgu_w, sgu_b, attn_sinks, conv_w, merge_g,
                        ln1_g, ln1_b, ln2_g, ln2_b, router_w, router_bias, sw_gate, sw_up, sw_down)
        if l == 0:
            donors = (jnp.zeros((T_ALL * ROWS_F32, LANES), F32), jnp.zeros((T_ALL * ROWS_PK, LANES), jnp.uint32),
                      jnp.zeros((N_EXPERTS, T_ALL), F32))
        else:
            donors = (xp, gw)
        base, xp, gw, klast, vlast, ctail, vrows = _prompt_mixer(l == 0, xin_p, p, donors)
        ck = cache_k[l].astype(BF).reshape(DEC_BATCH, WINDOW, LANES)
        cv = cache_v[l].astype(BF).reshape(DEC_BATCH, WINDOW, LANES)
        s0, s1 = state_conv[l][:, 0], state_conv[l][:, 1]
        base, xp, gw, knew, vnew, hc, vns = _sample_mixer(l == 0, xin_s, base, xp, gw, p, ck, cv, s0, s1)
        bstart, tok, gate = _dispatch_lists(gw)
        s2 = _experts(l, bstart, tok, gate, xp, ew_gate, ew_up, ew_down, base)
        xin_p = xin_s = s2
        outs[0].append(klast.reshape((BATCH,) + kv_shape))
        outs[1].append(vlast.reshape((BATCH,) + kv_shape))
        outs[2].append(ctail)
        outs[3].append(vrows)
        outs[4].append(jnp.concatenate([cache_k[l][:, 1:], knew.reshape((DEC_BATCH, 1) + kv_shape[1:])], axis=1))
        outs[5].append(jnp.concatenate([cache_v[l][:, 1:], vnew.reshape((DEC_BATCH, 1) + kv_shape[1:])], axis=1))
        outs[6].append(jnp.stack([s1, hc], axis=1))
        outs[7].append(vns[:, None])
    g2, b2 = ln2_g[DEPTH - 1].reshape(1, -1), ln2_b[DEPTH - 1].reshape(1, -1)
    y_prompt = _final_ln(s2, g2, b2, TM, T_PROMPT // TM, 0).reshape(BATCH, SEQ, D_MODEL)
    y_sample = _final_ln(s2, g2, b2, DEC_BATCH, 1, T_PROMPT // DEC_BATCH).reshape(DEC_BATCH, 1, D_MODEL)
    return (y_prompt, y_sample) + tuple(jnp.stack(o) for o in outs)
```

```python
import functools
import math

import numpy as np
import jax
import jax.numpy as jnp
from jax import lax
from jax.experimental import pallas as pl
from jax.experimental.pallas import tpu as pltpu

D_MODEL = 1024
BATCH = 2
SEQ = 8192
DEPTH = 2
DEC_BATCH = 128
PAST_LEN = 16384
W_A = 256
H_A = 4
HD_A = 64
CHUNK = 128
W_B = 512
HEAD_DIM = 64
N_HEADS = 8
KV_HEADS = 2
GQ = 4
WINDOW = 128
W_C = 256
CONV_W = 3
N_BUCKETS = 32
MAX_EXACT = 16
MAX_DIST = 128
N_EXPERTS = 64
TOP_K = 8
N_GROUPS = 8
TOPK_GROUPS = 4
GROUP_SIZE = N_EXPERTS // N_GROUPS
D_EXPERT = 256
ROUTE_SCALE = 2.5
ALPHA = (2 * DEPTH) ** 0.25
LN_EPS = 1e-5
IN_COLS = 2048

T_PROMPT = BATCH * SEQ
T_ALL = T_PROMPT + DEC_BATCH

LANES = 128
SUBLANES = 8
VMEM_LIMIT = 56 * 1024 * 1024

TM = 512
NBLK = TM // WINDOW
TILES_PER_SEQ = SEQ // TM
ROWS_F32 = D_MODEL // LANES
ROWS_PK = ROWS_F32 // 2
N_TGROUPS = 4
TG = T_ALL // N_TGROUPS
RB = 128
NEG = -1e30

BF = jnp.bfloat16
F32 = jnp.float32


def _ln(x, g, b):
    mu = jnp.mean(x, -1, keepdims=True)
    xc = x - mu
    var = jnp.mean(xc * xc, -1, keepdims=True)
    return xc * lax.rsqrt(var + LN_EPS) * g + b


def _rms(x, g):
    return x * lax.rsqrt(jnp.mean(x * x, -1, keepdims=True) + LN_EPS) * g


def _dot(a, b):
    return jnp.dot(a, b, preferred_element_type=F32)


def _dot_nt(a, b):
    return lax.dot_general(a, b, (((1,), (1,)), ((), ())), preferred_element_type=F32)


def _pack_pairs(lo, hi):
    lo_u = pltpu.bitcast(lo.astype(BF).astype(F32), jnp.uint32)
    hi_u = pltpu.bitcast(hi.astype(BF).astype(F32), jnp.uint32)
    return (hi_u & jnp.uint32(0xFFFF0000)) | (lo_u >> 16)


def _route(logits_t, rbias):
    n = logits_t.shape[1]
    scores = jax.nn.sigmoid(logits_t)
    biased = scores + rbias
    g3 = biased.reshape(N_GROUPS, GROUP_SIZE, n)
    sub = lax.broadcasted_iota(jnp.int32, g3.shape, 1)
    m1 = jnp.max(g3, axis=1, keepdims=True)
    first = jnp.min(jnp.where(g3 == m1, sub, GROUP_SIZE), axis=1, keepdims=True)
    m2 = jnp.max(jnp.where(sub == first, -jnp.inf, g3), axis=1, keepdims=True)
    gs = (m1 + m2).reshape(N_GROUPS, n)
    gi = lax.broadcasted_iota(jnp.int32, gs.shape, 0)
    grank = jnp.zeros(gs.shape, jnp.int32)
    for j in range(N_GROUPS):
        row = gs[j:j + 1]
        grank = grank + ((row > gs) | ((row == gs) & (j < gi))).astype(jnp.int32)
    gsel = (grank < TOPK_GROUPS).reshape(N_GROUPS, 1, n)
    masked = jnp.where(gsel, g3, -jnp.inf).reshape(N_EXPERTS, n)
    ei = lax.broadcasted_iota(jnp.int32, masked.shape, 0)
    rank = jnp.zeros(masked.shape, jnp.int32)
    for j in range(N_EXPERTS):
        row = masked[j:j + 1]
        rank = rank + ((row > masked) | ((row == masked) & (j < ei))).astype(jnp.int32)
    sel = rank < TOP_K
    w = jnp.where(sel, scores, 0.0)
    return w / jnp.sum(w, axis=0, keepdims=True) * ROUTE_SCALE


def _post_mixers(x, ya, yb, yc, mg_ref, wout_ref, ln1g_ref, ln1b_ref, wrt_ref, rb_ref,
                 swgu_ref, swd_ref, base_ref, xp_ref, gw_ref):
    n = x.shape[0]
    mg = mg_ref[...]
    cat = jnp.concatenate([_rms(ya, mg[:, :W_A]), _rms(yb, mg[:, W_A:W_A + W_B]),
                           _rms(yc, mg[:, W_A + W_B:])], axis=-1)
    m = _dot(cat.astype(BF), wout_ref[...])
    x1 = _ln(ALPHA * x + m, ln1g_ref[...], ln1b_ref[...])
    x1b = x1.astype(BF)
    gw_ref[...] = _route(_dot_nt(wrt_ref[...], x1b), rb_ref[...])
    hg = _dot(x1b, swgu_ref[...])
    hs = jax.nn.silu(hg[:, :D_EXPERT]) * hg[:, D_EXPERT:]
    base = ALPHA * x1 + _dot(hs.astype(BF), swd_ref[...])
    for c in range(ROWS_F32):
        base_ref[pl.ds(c, n, stride=ROWS_F32), :] = base[:, c * LANES:(c + 1) * LANES]
    half = D_MODEL // 2
    for j in range(ROWS_PK):
        xp_ref[pl.ds(j, n, stride=ROWS_PK), :] = _pack_pairs(
            x1[:, j * LANES:(j + 1) * LANES], x1[:, half + j * LANES:half + (j + 1) * LANES])


def _load_tokens(first_layer, xin_ref, n, lng_ref, lnb_ref):
    if first_layer:
        return xin_ref[...]
    s = jnp.concatenate([xin_ref[pl.ds(c, n, stride=ROWS_F32), :] for c in range(ROWS_F32)], axis=-1)
    return _ln(s, lng_ref[...], lnb_ref[...])


def _head_mask_sum(r):
    lane = lax.broadcasted_iota(jnp.int32, (CHUNK, W_A), 1) // HD_A
    out = r[0:CHUNK]
    for h in range(1, H_A):
        out = jnp.where(lane == h, r[h * CHUNK:(h + 1) * CHUNK], out)
    return out


def _prompt_mixer_kernel(first_layer, sinks_ref, *refs):
    (xin_ref, lng_ref, lnb_ref, win_ref, wout_ref, sguw_ref,
     sgub_ref, lnvg_ref, lnvb_ref, bprev_ref, bcur_ref, convw_ref, mg_ref, ln1g_ref,
     ln1b_ref, wrt_ref, rb_ref, swgu_ref, swd_ref,
     base_ref, xp_ref, gw_ref, klast_ref, vlast_ref, ctail_ref, vrows_ref,
     kprev_s, vprev_s, hcprev_s) = refs[(3 if first_layer else 2):]
    t = pl.program_id(1)
    is_first = t == 0

    @pl.when(is_first)
    def _():
        kprev_s[...] = jnp.zeros_like(kprev_s)
        vprev_s[...] = jnp.zeros_like(vprev_s)
        hcprev_s[...] = jnp.zeros_like(hcprev_s)

    x = _load_tokens(first_layer, xin_ref, TM, lng_ref, lnb_ref)
    z = _dot(x.astype(BF), win_ref[...])
    ua = jax.nn.gelu(z[:, 0:W_A])
    vn = _ln(jax.nn.gelu(z[:, W_A:2 * W_A]), lnvg_ref[...], lnvb_ref[...])
    q = z[:, 512:1024]
    k = z[:, 1024:1152]
    v = z[:, 1152:1280]
    gate_b = z[:, 1280:1536]
    gate_c = z[:, 1536:1792]
    h = z[:, 1792:2048]

    vnb = vn.astype(BF)
    sgub = sgub_ref[...]
    mixed = [_head_mask_sum(_dot(sguw_ref[...], vnb[c * CHUNK:(c + 1) * CHUNK])) + sgub for c in range(NBLK)]
    ya = ua * jnp.concatenate(mixed, axis=0)

    kb = k.astype(BF)
    vb = v.astype(BF)
    qb = q.astype(BF)
    scale = HEAD_DIM ** -0.5
    yb_blocks = []
    for c in range(NBLK):
        if c == 0:
            kp, vp = kprev_s[...], vprev_s[...]
            bsel = jnp.where(is_first, 1, 0)
        else:
            kp, vp = kb[(c - 1) * WINDOW:c * WINDOW], vb[(c - 1) * WINDOW:c * WINDOW]
            bsel = 0
        kc, vc = kb[c * WINDOW:(c + 1) * WINDOW], vb[c * WINDOW:(c + 1) * WINDOW]
        qc = qb[c * WINDOW:(c + 1) * WINDOW]
        outs = [None] * N_HEADS
        for g in range(KV_HEADS):
            gs = slice(g * HEAD_DIM, (g + 1) * HEAD_DIM)
            qg = jnp.concatenate([qc[:, (2 * r + g) * HEAD_DIM:(2 * r + g + 1) * HEAD_DIM] for r in range(GQ)], axis=0)
            sp = _dot_nt(qg, kp[:, gs]) * scale + bprev_ref[bsel, g]
            sc = _dot_nt(qg, kc[:, gs]) * scale + bcur_ref[g]
            for r in range(GQ):
                rs = slice(r * WINDOW, (r + 1) * WINDOW)
                sink = sinks_ref[g * GQ + r]
                spr, scr = sp[rs], sc[rs]
                mx = jnp.maximum(jnp.maximum(jnp.max(spr, -1, keepdims=True), jnp.max(scr, -1, keepdims=True)), sink)
                ep, ec = jnp.exp(spr - mx), jnp.exp(scr - mx)
                den = jnp.sum(ep, -1, keepdims=True) + jnp.sum(ec, -1, keepdims=True) + jnp.exp(sink - mx)
                o = _dot((ep / den).astype(BF), vp[:, gs]) + _dot((ec / den).astype(BF), vc[:, gs])
                outs[2 * r + g] = o
        yb_blocks.append(jnp.concatenate(outs, axis=-1))
    yb = jnp.concatenate(yb_blocks, axis=0)
    kprev_s[...] = kb[TM - WINDOW:]
    vprev_s[...] = vb[TM - WINDOW:]

    hc = gate_c * h
    prev = hcprev_s[...]
    row = lax.broadcasted_iota(jnp.int32, hc.shape, 0)
    h1 = jnp.where(row == 0, prev[7:8], pltpu.roll(hc, 1, 0))
    h2 = jnp.where(row == 0, prev[6:7], jnp.where(row == 1, prev[7:8], pltpu.roll(hc, 2, 0)))
    cw = convw_ref[...]
    yc = gate_b * (h2 * cw[0:1] + h1 * cw[1:2] + hc * cw[2:3])
    hcprev_s[...] = hc[TM - SUBLANES:]

    @pl.when(t == TILES_PER_SEQ - 1)
    def _():
        klast_ref[0] = k[TM - WINDOW:]
        vlast_ref[0] = v[TM - WINDOW:]
        ctail_ref[0] = hc[TM - (CONV_W - 1):]
        vrows_ref[0] = vn[TM - CHUNK:]

    _post_mixers(x, ya, yb, yc, mg_ref, wout_ref, ln1g_ref, ln1b_ref, wrt_ref, rb_ref, swgu_ref, swd_ref,
                 base_ref, xp_ref, gw_ref)


def _full(shape):
    nd = len(shape)
    return pl.BlockSpec(shape, lambda *_: (0,) * nd)


def _prompt_mixer(first_layer, xin, p, donors):
    nt = TILES_PER_SEQ
    tok = lambda b, t, *_: (b * nt + t, 0)
    anyspec = pl.BlockSpec(memory_space=pl.ANY)
    if first_layer:
        xin_spec = pl.BlockSpec((TM, D_MODEL), tok)
        aliases = {1: 0, 2: 1, 3: 2}
    else:
        xin_spec = pl.BlockSpec((TM * ROWS_F32, LANES), tok)
        aliases = {3: 0, 1: 1, 2: 2}
    in_specs = [anyspec] * len(donors) + [
        xin_spec, _full((1, D_MODEL)), _full((1, D_MODEL)),
        _full((D_MODEL, IN_COLS)), _full((D_MODEL, D_MODEL)), _full((H_A * CHUNK, CHUNK)),
        _full((CHUNK, W_A)), _full((1, W_A)), _full((1, W_A)),
        _full((2, KV_HEADS, GQ * WINDOW, WINDOW)), _full((KV_HEADS, GQ * WINDOW, WINDOW)),
        _full((CONV_W, W_C)), _full((1, D_MODEL)), _full((1, D_MODEL)), _full((1, D_MODEL)),
        _full((N_EXPERTS, D_MODEL)), _full((N_EXPERTS, 1)), _full((D_MODEL, 2 * D_EXPERT)),
        _full((D_EXPERT, D_MODEL)),
    ]
    per_seq = lambda b, t, *_: (b, 0, 0)
    out_shape = [
        jax.ShapeDtypeStruct((T_ALL * ROWS_F32, LANES), F32),
        jax.ShapeDtypeStruct((T_ALL * ROWS_PK, LANES), jnp.uint32),
        jax.ShapeDtypeStruct((N_EXPERTS, T_ALL), F32),
        jax.ShapeDtypeStruct((BATCH, WINDOW, KV_HEADS * HEAD_DIM), F32),
        jax.ShapeDtypeStruct((BATCH, WINDOW, KV_HEADS * HEAD_DIM), F32),
        jax.ShapeDtypeStruct((BATCH, CONV_W - 1, W_C), F32),
        jax.ShapeDtypeStruct((BATCH, CHUNK, W_A), F32),
    ]
    out_specs = [
        pl.BlockSpec((TM * ROWS_F32, LANES), tok),
        pl.BlockSpec((TM * ROWS_PK, LANES), tok),
        pl.BlockSpec((N_EXPERTS, TM), lambda b, t, *_: (0, b * nt + t)),
        pl.BlockSpec((1, WINDOW, KV_HEADS * HEAD_DIM), per_seq),
        pl.BlockSpec((1, WINDOW, KV_HEADS * HEAD_DIM), per_seq),
        pl.BlockSpec((1, CONV_W - 1, W_C), per_seq),
        pl.BlockSpec((1, CHUNK, W_A), per_seq),
    ]
    return pl.pallas_call(
        functools.partial(_prompt_mixer_kernel, first_layer),
        grid_spec=pltpu.PrefetchScalarGridSpec(
            num_scalar_prefetch=1, grid=(BATCH, nt), in_specs=in_specs, out_specs=out_specs,
            scratch_shapes=[pltpu.VMEM((WINDOW, KV_HEADS * HEAD_DIM), BF),
                            pltpu.VMEM((WINDOW, KV_HEADS * HEAD_DIM), BF),
                            pltpu.VMEM((SUBLANES, W_C), F32)]),
        out_shape=out_shape,
        input_output_aliases=aliases,
        compiler_params=pltpu.CompilerParams(dimension_semantics=("arbitrary", "arbitrary"),
                                             vmem_limit_bytes=VMEM_LIMIT),
        name="prompt_mixer",
    )(p["sinks"], *donors, xin, p["ln_prev_g"], p["ln_prev_b"], p["w_in"], p["w_out"], p["sgu_w"], p["sgu_b"],
      p["ln_v_g"], p["ln_v_b"], p["bias_prev"], p["bias_cur"], p["conv_w"], p["merge_g"], p["ln1_g"],
      p["ln1_b"], p["router_wt"], p["router_b"], p["sw_gu"], p["sw_d"])


def _half_sums(a, lane_lo):
    s0 = jnp.sum(jnp.where(lane_lo, a, 0.0), axis=1, keepdims=True)
    s1 = jnp.sum(jnp.where(lane_lo, 0.0, a), axis=1, keepdims=True)
    return jnp.where(lane_lo, s0, s1)


def _sample_mixer_kernel(first_layer, *refs):
    (xin_ref, lng_ref, lnb_ref, win_ref, wout_ref,
     sguw0_ref, sgub0_ref, lnvg_ref, lnvb_ref, ck_ref, cv_ref, bsamp_ref, bnew_ref,
     sinkrow_ref, s0_ref, s1_ref, convw_ref, mg_ref, ln1g_ref, ln1b_ref, wrt_ref,
     rb_ref, swgu_ref, swd_ref,
     base_ref, xp_ref, gw_ref, knew_ref, vnew_ref, hc_ref, vn_ref,
     q_s, yb_s, kn_s, vn_s) = refs[(3 if first_layer else 2):]
    n = DEC_BATCH
    x = _load_tokens(first_layer, xin_ref, n, lng_ref, lnb_ref)
    z = _dot(x.astype(BF), win_ref[...])
    ua = jax.nn.gelu(z[:, 0:W_A])
    vn = _ln(jax.nn.gelu(z[:, W_A:2 * W_A]), lnvg_ref[...], lnvb_ref[...])
    vn_ref[...] = vn
    ya = ua * (vn.astype(BF).astype(F32) * sguw0_ref[...] + sgub0_ref[...])
    k = z[:, 1024:1152]
    v = z[:, 1152:1280]
    knew_ref[...] = k
    vnew_ref[...] = v
    q_s[...] = z[:, 512:1024].astype(BF).astype(F32)
    kn_s[...] = k.astype(BF).astype(F32)
    vn_s[...] = v.astype(BF).astype(F32)
    scale = HEAD_DIM ** -0.5
    lane_lo = lax.broadcasted_iota(jnp.int32, (1, LANES), 1) < HEAD_DIM

    def per_seq_tile(bi, carry):
        b0 = pl.multiple_of(bi * SUBLANES, SUBLANES)
        q8 = q_s[pl.ds(b0, SUBLANES), :]
        kn8 = kn_s[pl.ds(b0, SUBLANES), :]
        vn8 = vn_s[pl.ds(b0, SUBLANES), :]
        rows = []
        for u in range(SUBLANES):
            kb = ck_ref[b0 + u].astype(F32)
            vb = cv_ref[b0 + u].astype(F32)
            kn = kn8[u:u + 1]
            vnw = vn8[u:u + 1]
            pieces = []
            for r in range(GQ):
                qp = q8[u:u + 1, r * LANES:(r + 1) * LANES]
                s = _half_sums(kb * qp, lane_lo) * scale + bsamp_ref[r]
                sn = _half_sums(kn * qp, lane_lo) * scale + bnew_ref[r]
                sink = sinkrow_ref[r]
                mx = jnp.maximum(jnp.maximum(jnp.max(s, axis=0, keepdims=True), sn), sink)
                e = jnp.exp(s - mx)
                en = jnp.exp(sn - mx)
                den = jnp.sum(e, axis=0, keepdims=True) + en + jnp.exp(sink - mx)
                p = (e / den).astype(BF).astype(F32)
                pn = (en / den).astype(BF).astype(F32)
                pieces.append(jnp.sum(p * vb, axis=0, keepdims=True) + pn * vnw)
            rows.append(jnp.concatenate(pieces, axis=1))
        yb_s[pl.ds(b0, SUBLANES), :] = jnp.concatenate(rows, axis=0)
        return carry

    lax.fori_loop(0, n // SUBLANES, per_seq_tile, 0)
    yb = yb_s[...]

    hc = z[:, 1536:1792] * z[:, 1792:2048]
    hc_ref[...] = hc
    cw = convw_ref[...]
    yc = z[:, 1280:1536] * (s0_ref[...] * cw[0:1] + s1_ref[...] * cw[1:2] + hc * cw[2:3])
    _post_mixers(x, ya, yb, yc, mg_ref, wout_ref, ln1g_ref, ln1b_ref, wrt_ref, rb_ref, swgu_ref, swd_ref,
                 base_ref, xp_ref, gw_ref)


def _sample_mixer(first_layer, xin, base, xp, gw, p, ck, cv, s0, s1):
    n = DEC_BATCH
    tail = T_PROMPT // n
    anyspec = pl.BlockSpec(memory_space=pl.ANY)
    if first_layer:
        xin_spec = pl.BlockSpec((n, D_MODEL), lambda i: (0, 0))
        lead, lead_specs, aliases = (base, xp, gw, xin), [anyspec] * 3, {0: 0, 1: 1, 2: 2}
    else:
        xin_spec = pl.BlockSpec((n * ROWS_F32, LANES), lambda i: (tail, 0))
        lead, lead_specs, aliases = (xp, gw, base), [anyspec] * 2, {2: 0, 0: 1, 1: 2}
    in_specs = lead_specs + [
        xin_spec, _full((1, D_MODEL)), _full((1, D_MODEL)),
        _full((D_MODEL, IN_COLS)), _full((D_MODEL, D_MODEL)), _full((1, W_A)), _full((1, W_A)),
        _full((1, W_A)), _full((1, W_A)),
        _full((n, WINDOW, LANES)), _full((n, WINDOW, LANES)),
        _full((GQ, WINDOW, LANES)), _full((GQ, 1, LANES)), _full((GQ, 1, LANES)),
        _full((n, W_C)), _full((n, W_C)),
        _full((CONV_W, W_C)), _full((1, D_MODEL)), _full((1, D_MODEL)), _full((1, D_MODEL)),
        _full((N_EXPERTS, D_MODEL)), _full((N_EXPERTS, 1)), _full((D_MODEL, 2 * D_EXPERT)),
        _full((D_EXPERT, D_MODEL)),
    ]
    out_shape = [
        jax.ShapeDtypeStruct(base.shape, base.dtype),
        jax.ShapeDtypeStruct(xp.shape, xp.dtype),
        jax.ShapeDtypeStruct(gw.shape, gw.dtype),
        jax.ShapeDtypeStruct((n, LANES), F32),
        jax.ShapeDtypeStruct((n, LANES), F32),
        jax.ShapeDtypeStruct((n, W_C), F32),
        jax.ShapeDtypeStruct((n, W_A), F32),
    ]
    out_specs = [
        pl.BlockSpec((n * ROWS_F32, LANES), lambda i: (tail, 0)),
        pl.BlockSpec((n * ROWS_PK, LANES), lambda i: (tail, 0)),
        pl.BlockSpec((N_EXPERTS, n), lambda i: (0, tail)),
        _full((n, LANES)), _full((n, LANES)), _full((n, W_C)), _full((n, W_A)),
    ]
    return pl.pallas_call(
        functools.partial(_sample_mixer_kernel, first_layer),
        grid=(1,), in_specs=in_specs, out_specs=out_specs, out_shape=out_shape,
        scratch_shapes=[pltpu.VMEM((n, W_B), F32), pltpu.VMEM((n, W_B), F32),
                        pltpu.VMEM((n, LANES), F32), pltpu.VMEM((n, LANES), F32)],
        input_output_aliases=aliases,
        compiler_params=pltpu.CompilerParams(dimension_semantics=("arbitrary",),
                                             vmem_limit_bytes=VMEM_LIMIT),
        name="sample_mixer",
    )(*lead, p["ln_prev_g"], p["ln_prev_b"], p["w_in"], p["w_out"], p["sgu_w0"], p["sgu_b0"],
      p["ln_v_g"], p["ln_v_b"], ck, cv, p["bias_samp"], p["bias_new"], p["sink_row"], s0, s1,
      p["conv_w"], p["merge_g"], p["ln1_g"], p["ln1_b"], p["router_wt"], p["router_b"], p["sw_gu"],
      p["sw_d"])


LIST_LEN = -(-(TG * TOP_K + RB) // 1024) * 1024
UNROLL = 8


def _expert_kernel(off_ref, tok_ref, gate_ref, xp_ref, wg_ref, wu_ref, wdn_ref, base_any, out_any,
                   acc, xs, xbf, ybuf, wgu_s, wd_s, sem):
    g = pl.program_id(0)
    e = pl.program_id(1)
    rows = TG * ROWS_F32

    def group_copy(to_vmem):
        hbm = (base_any if to_vmem else out_any).at[pl.ds(g * rows, rows)]
        vm = acc.at[pl.ds(0, rows)]
        return pltpu.make_async_copy(hbm, vm, sem) if to_vmem else pltpu.make_async_copy(vm, hbm, sem)

    @pl.when(e == 0)
    def _():
        cp = group_copy(True)
        cp.start()
        acc[pl.ds(rows, ROWS_F32), :] = jnp.zeros((ROWS_F32, LANES), F32)
        xs[...] = jnp.zeros_like(xs)
        cp.wait()

    wgu_s[:, :D_EXPERT] = wg_ref[0, 0].astype(BF)
    wgu_s[:, D_EXPERT:] = wu_ref[0, 0].astype(BF)
    wd_s[...] = wdn_ref[0, 0].astype(BF)

    start = off_ref[g * (N_EXPERTS + 1) + e]
    end = off_ref[g * (N_EXPERTS + 1) + e + 1]
    nblocks = (end - start + RB - 1) // RB

    def block(bi, carry):
        r0 = start + bi * RB
        nrows = jnp.minimum(end - r0, RB)
        ntrips = (nrows + UNROLL - 1) // UNROLL

        def gather(i, c):
            for u in range(UNROLL):
                r = i * UNROLL + u
                src = pl.multiple_of(tok_ref[r0 + r] * ROWS_PK, ROWS_PK)
                dst = pl.multiple_of(r * ROWS_PK, ROWS_PK)
                xs[pl.ds(dst, ROWS_PK), :] = xp_ref[pl.ds(src, ROWS_PK), :]
            return c

        lax.fori_loop(0, ntrips, gather, 0)
        half = D_MODEL // 2
        for j in range(ROWS_PK):
            u32 = xs[pl.ds(j, RB, stride=ROWS_PK), :]
            xbf[:, j * LANES:(j + 1) * LANES] = pltpu.bitcast(u32 << 16, F32).astype(BF)
            xbf[:, half + j * LANES:half + (j + 1) * LANES] = pltpu.bitcast(
                u32 & jnp.uint32(0xFFFF0000), F32).astype(BF)
        hg = _dot(xbf[...], wgu_s[...])
        hid = jax.nn.silu(hg[:, :D_EXPERT]) * hg[:, D_EXPERT:]
        y = _dot(hid.astype(BF), wd_s[...])
        for c in range(ROWS_F32):
            ybuf[pl.ds(c, RB, stride=ROWS_F32), :] = y[:, c * LANES:(c + 1) * LANES]

        def scatter(checked, i, c):
            dsts, vals = [], []
            for u in range(UNROLL):
                r = i * UNROLL + u
                tk, gt = tok_ref[r0 + r], gate_ref[r0 + r]
                if checked:
                    valid = r < nrows
                    tk, gt = jnp.where(valid, tk, TG), jnp.where(valid, gt, 0.0)
                dst = pl.multiple_of(tk * ROWS_F32, ROWS_F32)
                src = pl.multiple_of(r * ROWS_F32, ROWS_F32)
                dsts.append(dst)
                vals.append(acc[pl.ds(dst, ROWS_F32), :] + gt * ybuf[pl.ds(src, ROWS_F32), :])
            for dst, val in zip(dsts, vals):
                acc[pl.ds(dst, ROWS_F32), :] = val
            return c

        nfull = nrows // UNROLL
        lax.fori_loop(0, nfull, functools.partial(scatter, False), 0)
        lax.fori_loop(nfull, ntrips, functools.partial(scatter, True), 0)
        return carry

    lax.fori_loop(0, nblocks, block, 0)

    @pl.when(e == N_EXPERTS - 1)
    def _():
        cp = group_copy(False)
        cp.start()
        cp.wait()


def _experts(layer, off, tok, gate, xp, wg, wu, wdn, base):
    lists = pl.BlockSpec((LIST_LEN,), lambda g, e, *_: (g,), memory_space=pltpu.SMEM)
    anyspec = pl.BlockSpec(memory_space=pl.ANY)
    in_specs = [
        lists, lists,
        pl.BlockSpec((TG * ROWS_PK, LANES), lambda g, e, *_: (g, 0)),
        pl.BlockSpec((1, 1, D_MODEL, D_EXPERT), lambda g, e, *_: (layer, e, 0, 0)),
        pl.BlockSpec((1, 1, D_MODEL, D_EXPERT), lambda g, e, *_: (layer, e, 0, 0)),
        pl.BlockSpec((1, 1, D_EXPERT, D_MODEL), lambda g, e, *_: (layer, e, 0, 0)),
        anyspec,
    ]
    return pl.pallas_call(
        _expert_kernel,
        grid_spec=pltpu.PrefetchScalarGridSpec(
            num_scalar_prefetch=1, grid=(N_TGROUPS, N_EXPERTS), in_specs=in_specs,
            out_specs=anyspec,
            scratch_shapes=[pltpu.VMEM(((TG + 1) * ROWS_F32, LANES), F32),
                            pltpu.VMEM((RB * ROWS_PK, LANES), jnp.uint32),
                            pltpu.VMEM((RB, D_MODEL), BF),
                            pltpu.VMEM((RB * ROWS_F32, LANES), F32),
                            pltpu.VMEM((D_MODEL, 2 * D_EXPERT), BF),
                            pltpu.VMEM((D_EXPERT, D_MODEL), BF),
                            pltpu.SemaphoreType.DMA]),
        out_shape=jax.ShapeDtypeStruct(base.shape, base.dtype),
        input_output_aliases={7: 0},
        compiler_params=pltpu.CompilerParams(dimension_semantics=("arbitrary", "arbitrary"),
                                             vmem_limit_bytes=VMEM_LIMIT),
        name="experts",
    )(off, tok, gate, xp, wg, wu, wdn, base)


def _dispatch_lists(gw):
    chosen = (gw > 0).reshape(N_EXPERTS, N_TGROUPS, TG).transpose(1, 0, 2)
    counts = jnp.sum(chosen, axis=-1, dtype=jnp.int32)
    off = jnp.concatenate([jnp.zeros((N_TGROUPS, 1), jnp.int32), jnp.cumsum(counts, axis=1)], axis=1)
    flat = chosen.reshape(N_TGROUPS, N_EXPERTS * TG)
    idx = jax.vmap(lambda f: jnp.nonzero(f, size=TG * TOP_K, fill_value=0)[0])(flat).astype(jnp.int32)
    gflat = gw.reshape(N_EXPERTS, N_TGROUPS, TG).transpose(1, 0, 2).reshape(N_TGROUPS, N_EXPERTS * TG)
    gate = jnp.take_along_axis(gflat, idx, axis=1)
    pad = ((0, 0), (0, LIST_LEN - TG * TOP_K))
    tok = jnp.pad(idx % TG, pad).reshape(-1)
    return off.reshape(-1), tok, jnp.pad(gate, pad).reshape(-1)


def _final_ln_kernel(n, s_ref, g_ref, b_ref, o_ref):
    o_ref[...] = _load_tokens(False, s_ref, n, g_ref, b_ref)


def _final_ln(s2, g, b, n, steps, first_block):
    return pl.pallas_call(
        functools.partial(_final_ln_kernel, n),
        grid=(steps,),
        in_specs=[pl.BlockSpec((n * ROWS_F32, LANES), lambda i: (first_block + i, 0)),
                  _full((1, D_MODEL)), _full((1, D_MODEL))],
        out_specs=pl.BlockSpec((n, D_MODEL), lambda i: (i, 0)),
        out_shape=jax.ShapeDtypeStruct((n * steps, D_MODEL), F32),
        compiler_params=pltpu.CompilerParams(dimension_semantics=("arbitrary",)),
        name="final_norm",
    )(s2, g, b)


_HEAD_AT = [(p % 2) * GQ + p // 2 for p in range(N_HEADS)]


def _bucket(dist):
    n = np.maximum(dist, 0)
    nf = np.maximum(n, 1).astype(np.float32)
    large = MAX_EXACT + (np.log(nf / np.float32(MAX_EXACT)) / np.float32(math.log(MAX_DIST / MAX_EXACT))
                         * np.float32(N_BUCKETS - MAX_EXACT)).astype(np.int32)
    return np.where(n < MAX_EXACT, n, np.minimum(large, N_BUCKETS - 1)).astype(np.int32)


def _bias_tables(rel_bias):
    qi = np.arange(WINDOW)[:, None]
    kk = np.arange(WINDOW)[None, :]

    def table(dist):
        valid = (dist >= 0) & (dist < WINDOW)
        t = jnp.where(valid[..., None], rel_bias[_bucket(dist)], NEG)
        return jnp.moveaxis(t, -1, 0).reshape(KV_HEADS, GQ * WINDOW, WINDOW)

    prev = table(WINDOW + qi - kk)
    bias_prev = jnp.stack([prev, jnp.full_like(prev, NEG)])
    bias_cur = table(qi - kk)
    dist_s = WINDOW - np.arange(WINDOW)
    col = jnp.where(((dist_s >= 0) & (dist_s < WINDOW))[:, None], rel_bias[_bucket(dist_s)], NEG)
    pairs = [jnp.concatenate([jnp.broadcast_to(col[:, r:r + 1], (WINDOW, HEAD_DIM)),
                              jnp.broadcast_to(col[:, GQ + r:GQ + r + 1], (WINDOW, HEAD_DIM))], axis=1)
             for r in range(GQ)]
    new = rel_bias[0]
    bias_new = jnp.stack([_pair_row(new, r) for r in range(GQ)])
    return bias_prev, bias_cur, jnp.stack(pairs), bias_new


def _pair_row(per_head, r):
    return jnp.concatenate([jnp.broadcast_to(per_head[r], (1, HEAD_DIM)),
                            jnp.broadcast_to(per_head[GQ + r], (1, HEAD_DIM))], axis=1)


def _prep_layer(l, tables, w_in, w_out, ln_v_g, ln_v_b, sgu_w, sgu_b, attn_sinks, conv_w, merge_g,
                ln1_g, ln1_b, ln2_g, ln2_b, router_w, router_bias, sw_gate, sw_up, sw_down):
    q0 = 2 * W_A
    qcols = np.concatenate([q0 + h * HEAD_DIM + np.arange(HEAD_DIM) for h in _HEAD_AT])
    cols = np.concatenate([np.arange(q0), qcols, np.arange(q0 + W_B, IN_COLS)])
    rows = np.concatenate([np.arange(W_A), qcols - q0 + W_A, np.arange(W_A + W_B, D_MODEL)])
    row = lambda a: a.reshape(1, -1)
    prev = max(l - 1, 0)
    bias_prev, bias_cur, bias_samp, bias_new = tables
    return dict(
        sinks=attn_sinks[l],
        ln_prev_g=row(ln2_g[prev]), ln_prev_b=row(ln2_b[prev]),
        w_in=w_in[l][:, cols].astype(BF), w_out=w_out[l][rows].astype(BF),
        sgu_w=jnp.tril(sgu_w[l]).reshape(H_A * CHUNK, CHUNK).astype(BF),
        sgu_b=jnp.repeat(sgu_b[l].T, HD_A, axis=1),
        sgu_w0=row(jnp.repeat(sgu_w[l][:, 0, 0].astype(BF).astype(F32), HD_A)),
        sgu_b0=row(jnp.repeat(sgu_b[l][:, 0], HD_A)),
        ln_v_g=row(ln_v_g[l]), ln_v_b=row(ln_v_b[l]),
        bias_prev=bias_prev, bias_cur=bias_cur, bias_samp=bias_samp, bias_new=bias_new,
        sink_row=jnp.stack([_pair_row(attn_sinks[l], r) for r in range(GQ)]),
        conv_w=conv_w[l].T, merge_g=row(merge_g[l][rows]), ln1_g=row(ln1_g[l]), ln1_b=row(ln1_b[l]),
        router_wt=router_w[l].T.astype(BF), router_b=router_bias[l].reshape(N_EXPERTS, 1),
        sw_gu=jnp.concatenate([sw_gate[l], sw_up[l]], axis=1).astype(BF), sw_d=sw_down[l].astype(BF),
    )


def kernel(x_prompt, x_sample, cache_k, cache_v, state_conv, rel_bias, w_in, w_out, ln_v_g, ln_v_b,
           sgu_w, sgu_b, attn_sinks, conv_w, merge_g, ln1_g, ln1_b, ln2_g, ln2_b, router_w, router_bias,
           ew_gate, ew_up, ew_down, sw_gate, sw_up, sw_down):
    tables = _bias_tables(rel_bias)
    xin_p = x_prompt.reshape(T_PROMPT, D_MODEL)
    xin_s = x_sample.reshape(DEC_BATCH, D_MODEL)
    kv_shape = (WINDOW, KV_HEADS, HEAD_DIM)
    outs = [[] for _ in range(8)]
    for l in range(DEPTH):
        p = _prep_layer(l, tables, w_in, w_out, ln_v_g, ln_v_b, sgu_w, sgu_b, attn_sinks, conv_w, merge_g,
                        ln1_g, ln1_b, ln2_g, ln2_b, router_w, router_bias, sw_gate, sw_up, sw_down)
        if l == 0:
            donors = (jnp.zeros((T_ALL * ROWS_F32, LANES), F32), jnp.zeros((T_ALL * ROWS_PK, LANES), jnp.uint32),
                      jnp.zeros((N_EXPERTS, T_ALL), F32))
        else:
            donors = (xp, gw)
        base, xp, gw, klast, vlast, ctail, vrows = _prompt_mixer(l == 0, xin_p, p, donors)
        ck = cache_k[l].astype(BF).reshape(DEC_BATCH, WINDOW, LANES)
        cv = cache_v[l].astype(BF).reshape(DEC_BATCH, WINDOW, LANES)
        s0, s1 = state_conv[l][:, 0], state_conv[l][:, 1]
        base, xp, gw, knew, vnew, hc, vns = _sample_mixer(l == 0, xin_s, base, xp, gw, p, ck, cv, s0, s1)
        off, tok, gate = _dispatch_lists(gw)
        s2 = _experts(l, off, tok, gate, xp, ew_gate, ew_up, ew_down, base)
        xin_p = xin_s = s2
        outs[0].append(klast.reshape((BATCH,) + kv_shape))
        outs[1].append(vlast.reshape((BATCH,) + kv_shape))
        outs[2].append(ctail)
        outs[3].append(vrows)
        outs[4].append(jnp.concatenate([cache_k[l][:, 1:], knew.reshape((DEC_BATCH, 1) + kv_shape[1:])], axis=1))
        outs[5].append(jnp.concatenate([cache_v[l][:, 1:], vnew.reshape((DEC_BATCH, 1) + kv_shape[1:])], axis=1))
        outs[6].append(jnp.stack([s1, hc], axis=1))
        outs[7].append(vns[:, None])
    g2, b2 = ln2_g[DEPTH - 1].reshape(1, -1), ln2_b[DEPTH - 1].reshape(1, -1)
    y_prompt = _final_ln(s2, g2, b2, TM, T_PROMPT // TM, 0).reshape(BATCH, SEQ, D_MODEL)
    y_sample = _final_ln(s2, g2, b2, DEC_BATCH, 1, T_PROMPT // DEC_BATCH).reshape(DEC_BATCH, 1, D_MODEL)
    return (y_prompt, y_sample) + tuple(jnp.stack(o) for o in outs)
```

```python
import functools
import math

import numpy as np
import jax
import jax.numpy as jnp
from jax import lax
from jax.experimental import pallas as pl
from jax.experimental.pallas import tpu as pltpu

D_MODEL = 1024
BATCH = 2
SEQ = 8192
DEPTH = 2
DEC_BATCH = 128
PAST_LEN = 16384
W_A = 256
H_A = 4
HD_A = 64
CHUNK = 128
W_B = 512
HEAD_DIM = 64
N_HEADS = 8
KV_HEADS = 2
GQ = 4
WINDOW = 128
W_C = 256
CONV_W = 3
N_BUCKETS = 32
MAX_EXACT = 16
MAX_DIST = 128
N_EXPERTS = 64
TOP_K = 8
N_GROUPS = 8
TOPK_GROUPS = 4
GROUP_SIZE = N_EXPERTS // N_GROUPS
D_EXPERT = 256
ROUTE_SCALE = 2.5
ALPHA = (2 * DEPTH) ** 0.25
LN_EPS = 1e-5
IN_COLS = 2048

T_PROMPT = BATCH * SEQ
T_ALL = T_PROMPT + DEC_BATCH

LANES = 128
SUBLANES = 8
VMEM_LIMIT = 56 * 1024 * 1024

TM = 512
NBLK = TM // WINDOW
TILES_PER_SEQ = SEQ // TM
ROWS_F32 = D_MODEL // LANES
ROWS_PK = ROWS_F32 // 2
N_TGROUPS = 4
TG = T_ALL // N_TGROUPS
RB = 256
SUB = 128
NEG = -1e30

BF = jnp.bfloat16
F32 = jnp.float32


def _ln(x, g, b):
    mu = jnp.mean(x, -1, keepdims=True)
    xc = x - mu
    var = jnp.mean(xc * xc, -1, keepdims=True)
    return xc * lax.rsqrt(var + LN_EPS) * g + b


def _rms(x, g):
    return x * lax.rsqrt(jnp.mean(x * x, -1, keepdims=True) + LN_EPS) * g


def _dot(a, b):
    return jnp.dot(a, b, preferred_element_type=F32)


def _dot_nt(a, b):
    return lax.dot_general(a, b, (((1,), (1,)), ((), ())), preferred_element_type=F32)


def _pack_pairs(lo, hi):
    lo_u = pltpu.bitcast(lo.astype(BF).astype(F32), jnp.uint32)
    hi_u = pltpu.bitcast(hi.astype(BF).astype(F32), jnp.uint32)
    return (hi_u & jnp.uint32(0xFFFF0000)) | (lo_u >> 16)


def _route(logits_t, rbias):
    n = logits_t.shape[1]
    scores = jax.nn.sigmoid(logits_t)
    biased = scores + rbias
    g3 = biased.reshape(N_GROUPS, GROUP_SIZE, n)
    sub = lax.broadcasted_iota(jnp.int32, g3.shape, 1)
    m1 = jnp.max(g3, axis=1, keepdims=True)
    first = jnp.min(jnp.where(g3 == m1, sub, GROUP_SIZE), axis=1, keepdims=True)
    m2 = jnp.max(jnp.where(sub == first, -jnp.inf, g3), axis=1, keepdims=True)
    gs = (m1 + m2).reshape(N_GROUPS, n)
    gi = lax.broadcasted_iota(jnp.int32, gs.shape, 0)
    grank = jnp.zeros(gs.shape, jnp.int32)
    for j in range(N_GROUPS):
        row = gs[j:j + 1]
        grank = grank + ((row > gs) | ((row == gs) & (j < gi))).astype(jnp.int32)
    gsel = (grank < TOPK_GROUPS).reshape(N_GROUPS, 1, n)
    masked = jnp.where(gsel, g3, -jnp.inf).reshape(N_EXPERTS, n)
    ei = lax.broadcasted_iota(jnp.int32, masked.shape, 0)
    rank = jnp.zeros(masked.shape, jnp.int32)
    for j in range(N_EXPERTS):
        row = masked[j:j + 1]
        rank = rank + ((row > masked) | ((row == masked) & (j < ei))).astype(jnp.int32)
    sel = rank < TOP_K
    w = jnp.where(sel, scores, 0.0)
    return w / jnp.sum(w, axis=0, keepdims=True) * ROUTE_SCALE


def _post_mixers(x, ya, yb, yc, mg_ref, wout_ref, ln1g_ref, ln1b_ref, wrt_ref, rb_ref,
                 swgu_ref, swd_ref, base_ref, xp_ref, gw_ref):
    n = x.shape[0]
    mg = mg_ref[...]
    cat = jnp.concatenate([_rms(ya, mg[:, :W_A]), _rms(yb, mg[:, W_A:W_A + W_B]),
                           _rms(yc, mg[:, W_A + W_B:])], axis=-1)
    m = _dot(cat.astype(BF), wout_ref[...])
    x1 = _ln(ALPHA * x + m, ln1g_ref[...], ln1b_ref[...])
    x1b = x1.astype(BF)
    gw_ref[...] = _route(_dot_nt(wrt_ref[...], x1b), rb_ref[...])
    hg = _dot(x1b, swgu_ref[...])
    hs = jax.nn.silu(hg[:, :D_EXPERT]) * hg[:, D_EXPERT:]
    base = ALPHA * x1 + _dot(hs.astype(BF), swd_ref[...])
    for c in range(ROWS_F32):
        base_ref[pl.ds(c, n, stride=ROWS_F32), :] = base[:, c * LANES:(c + 1) * LANES]
    half = D_MODEL // 2
    for j in range(ROWS_PK):
        xp_ref[pl.ds(j, n, stride=ROWS_PK), :] = _pack_pairs(
            x1[:, j * LANES:(j + 1) * LANES], x1[:, half + j * LANES:half + (j + 1) * LANES])


def _load_tokens(first_layer, xin_ref, n, lng_ref, lnb_ref):
    if first_layer:
        return xin_ref[...]
    s = jnp.concatenate([xin_ref[pl.ds(c, n, stride=ROWS_F32), :] for c in range(ROWS_F32)], axis=-1)
    return _ln(s, lng_ref[...], lnb_ref[...])


def _head_mask_sum(r):
    lane = lax.broadcasted_iota(jnp.int32, (CHUNK, W_A), 1) // HD_A
    out = r[0:CHUNK]
    for h in range(1, H_A):
        out = jnp.where(lane == h, r[h * CHUNK:(h + 1) * CHUNK], out)
    return out


def _prompt_mixer_kernel(first_layer, sinks_ref, *refs):
    (xin_ref, lng_ref, lnb_ref, win_ref, wout_ref, sguw_ref,
     sgub_ref, lnvg_ref, lnvb_ref, bprev_ref, bcur_ref, convw_ref, mg_ref, ln1g_ref,
     ln1b_ref, wrt_ref, rb_ref, swgu_ref, swd_ref,
     base_ref, xp_ref, gw_ref, klast_ref, vlast_ref, ctail_ref, vrows_ref,
     kprev_s, vprev_s, hcprev_s) = refs[(3 if first_layer else 2):]
    t = pl.program_id(1)
    is_first = t == 0

    @pl.when(is_first)
    def _():
        kprev_s[...] = jnp.zeros_like(kprev_s)
        vprev_s[...] = jnp.zeros_like(vprev_s)
        hcprev_s[...] = jnp.zeros_like(hcprev_s)

    x = _load_tokens(first_layer, xin_ref, TM, lng_ref, lnb_ref)
    z = _dot(x.astype(BF), win_ref[...])
    ua = jax.nn.gelu(z[:, 0:W_A])
    vn = _ln(jax.nn.gelu(z[:, W_A:2 * W_A]), lnvg_ref[...], lnvb_ref[...])
    q = z[:, 512:1024]
    k = z[:, 1024:1152]
    v = z[:, 1152:1280]
    gate_b = z[:, 1280:1536]
    gate_c = z[:, 1536:1792]
    h = z[:, 1792:2048]

    vnb = vn.astype(BF)
    sgub = sgub_ref[...]
    mixed = [_head_mask_sum(_dot(sguw_ref[...], vnb[c * CHUNK:(c + 1) * CHUNK])) + sgub for c in range(NBLK)]
    ya = ua * jnp.concatenate(mixed, axis=0)

    kb = k.astype(BF)
    vb = v.astype(BF)
    qb = q.astype(BF)
    scale = HEAD_DIM ** -0.5
    yb_blocks = []
    for c in range(NBLK):
        if c == 0:
            kp, vp = kprev_s[...], vprev_s[...]
            bsel = jnp.where(is_first, 1, 0)
        else:
            kp, vp = kb[(c - 1) * WINDOW:c * WINDOW], vb[(c - 1) * WINDOW:c * WINDOW]
            bsel = 0
        kc, vc = kb[c * WINDOW:(c + 1) * WINDOW], vb[c * WINDOW:(c + 1) * WINDOW]
        qc = qb[c * WINDOW:(c + 1) * WINDOW]
        outs = [None] * N_HEADS
        for g in range(KV_HEADS):
            gs = slice(g * HEAD_DIM, (g + 1) * HEAD_DIM)
            qg = jnp.concatenate([qc[:, (2 * r + g) * HEAD_DIM:(2 * r + g + 1) * HEAD_DIM] for r in range(GQ)], axis=0)
            sp = _dot_nt(qg, kp[:, gs]) * scale + bprev_ref[bsel, g]
            sc = _dot_nt(qg, kc[:, gs]) * scale + bcur_ref[g]
            for r in range(GQ):
                rs = slice(r * WINDOW, (r + 1) * WINDOW)
                sink = sinks_ref[g * GQ + r]
                spr, scr = sp[rs], sc[rs]
                mx = jnp.maximum(jnp.maximum(jnp.max(spr, -1, keepdims=True), jnp.max(scr, -1, keepdims=True)), sink)
                ep, ec = jnp.exp(spr - mx), jnp.exp(scr - mx)
                den = jnp.sum(ep, -1, keepdims=True) + jnp.sum(ec, -1, keepdims=True) + jnp.exp(sink - mx)
                o = _dot((ep / den).astype(BF), vp[:, gs]) + _dot((ec / den).astype(BF), vc[:, gs])
                outs[2 * r + g] = o
        yb_blocks.append(jnp.concatenate(outs, axis=-1))
    yb = jnp.concatenate(yb_blocks, axis=0)
    kprev_s[...] = kb[TM - WINDOW:]
    vprev_s[...] = vb[TM - WINDOW:]

    hc = gate_c * h
    prev = hcprev_s[...]
    row = lax.broadcasted_iota(jnp.int32, hc.shape, 0)
    h1 = jnp.where(row == 0, prev[7:8], pltpu.roll(hc, 1, 0))
    h2 = jnp.where(row == 0, prev[6:7], jnp.where(row == 1, prev[7:8], pltpu.roll(hc, 2, 0)))
    cw = convw_ref[...]
    yc = gate_b * (h2 * cw[0:1] + h1 * cw[1:2] + hc * cw[2:3])
    hcprev_s[...] = hc[TM - SUBLANES:]

    @pl.when(t == TILES_PER_SEQ - 1)
    def _():
        klast_ref[0] = k[TM - WINDOW:]
        vlast_ref[0] = v[TM - WINDOW:]
        ctail_ref[0] = hc[TM - (CONV_W - 1):]
        vrows_ref[0] = vn[TM - CHUNK:]

    _post_mixers(x, ya, yb, yc, mg_ref, wout_ref, ln1g_ref, ln1b_ref, wrt_ref, rb_ref, swgu_ref, swd_ref,
                 base_ref, xp_ref, gw_ref)


def _full(shape):
    nd = len(shape)
    return pl.BlockSpec(shape, lambda *_: (0,) * nd)


def _prompt_mixer(first_layer, xin, p, donors):
    nt = TILES_PER_SEQ
    tok = lambda b, t, *_: (b * nt + t, 0)
    anyspec = pl.BlockSpec(memory_space=pl.ANY)
    if first_layer:
        xin_spec = pl.BlockSpec((TM, D_MODEL), tok)
        aliases = {1: 0, 2: 1, 3: 2}
    else:
        xin_spec = pl.BlockSpec((TM * ROWS_F32, LANES), tok)
        aliases = {3: 0, 1: 1, 2: 2}
    in_specs = [anyspec] * len(donors) + [
        xin_spec, _full((1, D_MODEL)), _full((1, D_MODEL)),
        _full((D_MODEL, IN_COLS)), _full((D_MODEL, D_MODEL)), _full((H_A * CHUNK, CHUNK)),
        _full((CHUNK, W_A)), _full((1, W_A)), _full((1, W_A)),
        _full((2, KV_HEADS, GQ * WINDOW, WINDOW)), _full((KV_HEADS, GQ * WINDOW, WINDOW)),
        _full((CONV_W, W_C)), _full((1, D_MODEL)), _full((1, D_MODEL)), _full((1, D_MODEL)),
        _full((N_EXPERTS, D_MODEL)), _full((N_EXPERTS, 1)), _full((D_MODEL, 2 * D_EXPERT)),
        _full((D_EXPERT, D_MODEL)),
    ]
    per_seq = lambda b, t, *_: (b, 0, 0)
    out_shape = [
        jax.ShapeDtypeStruct((T_ALL * ROWS_F32, LANES), F32),
        jax.ShapeDtypeStruct((T_ALL * ROWS_PK, LANES), jnp.uint32),
        jax.ShapeDtypeStruct((N_EXPERTS, T_ALL), F32),
        jax.ShapeDtypeStruct((BATCH, WINDOW, KV_HEADS * HEAD_DIM), F32),
        jax.ShapeDtypeStruct((BATCH, WINDOW, KV_HEADS * HEAD_DIM), F32),
        jax.ShapeDtypeStruct((BATCH, CONV_W - 1, W_C), F32),
        jax.ShapeDtypeStruct((BATCH, CHUNK, W_A), F32),
    ]
    out_specs = [
        pl.BlockSpec((TM * ROWS_F32, LANES), tok),
        pl.BlockSpec((TM * ROWS_PK, LANES), tok),
        pl.BlockSpec((N_EXPERTS, TM), lambda b, t, *_: (0, b * nt + t)),
        pl.BlockSpec((1, WINDOW, KV_HEADS * HEAD_DIM), per_seq),
        pl.BlockSpec((1, WINDOW, KV_HEADS * HEAD_DIM), per_seq),
        pl.BlockSpec((1, CONV_W - 1, W_C), per_seq),
        pl.BlockSpec((1, CHUNK, W_A), per_seq),
    ]
    return pl.pallas_call(
        functools.partial(_prompt_mixer_kernel, first_layer),
        grid_spec=pltpu.PrefetchScalarGridSpec(
            num_scalar_prefetch=1, grid=(BATCH, nt), in_specs=in_specs, out_specs=out_specs,
            scratch_shapes=[pltpu.VMEM((WINDOW, KV_HEADS * HEAD_DIM), BF),
                            pltpu.VMEM((WINDOW, KV_HEADS * HEAD_DIM), BF),
                            pltpu.VMEM((SUBLANES, W_C), F32)]),
        out_shape=out_shape,
        input_output_aliases=aliases,
        compiler_params=pltpu.CompilerParams(dimension_semantics=("arbitrary", "arbitrary"),
                                             vmem_limit_bytes=VMEM_LIMIT),
        name="prompt_mixer",
    )(p["sinks"], *donors, xin, p["ln_prev_g"], p["ln_prev_b"], p["w_in"], p["w_out"], p["sgu_w"], p["sgu_b"],
      p["ln_v_g"], p["ln_v_b"], p["bias_prev"], p["bias_cur"], p["conv_w"], p["merge_g"], p["ln1_g"],
      p["ln1_b"], p["router_wt"], p["router_b"], p["sw_gu"], p["sw_d"])


def _half_sums(a, lane_lo):
    s0 = jnp.sum(jnp.where(lane_lo, a, 0.0), axis=1, keepdims=True)
    s1 = jnp.sum(jnp.where(lane_lo, 0.0, a), axis=1, keepdims=True)
    return jnp.where(lane_lo, s0, s1)


def _sample_mixer_kernel(first_layer, *refs):
    (xin_ref, lng_ref, lnb_ref, win_ref, wout_ref,
     sguw0_ref, sgub0_ref, lnvg_ref, lnvb_ref, ck_ref, cv_ref, bsamp_ref, bnew_ref,
     sinkrow_ref, s0_ref, s1_ref, convw_ref, mg_ref, ln1g_ref, ln1b_ref, wrt_ref,
     rb_ref, swgu_ref, swd_ref,
     base_ref, xp_ref, gw_ref, knew_ref, vnew_ref, hc_ref, vn_ref,
     q_s, yb_s, kn_s, vn_s) = refs[(3 if first_layer else 2):]
    n = DEC_BATCH
    x = _load_tokens(first_layer, xin_ref, n, lng_ref, lnb_ref)
    z = _dot(x.astype(BF), win_ref[...])
    ua = jax.nn.gelu(z[:, 0:W_A])
    vn = _ln(jax.nn.gelu(z[:, W_A:2 * W_A]), lnvg_ref[...], lnvb_ref[...])
    vn_ref[...] = vn
    ya = ua * (vn.astype(BF).astype(F32) * sguw0_ref[...] + sgub0_ref[...])
    k = z[:, 1024:1152]
    v = z[:, 1152:1280]
    knew_ref[...] = k
    vnew_ref[...] = v
    q_s[...] = z[:, 512:1024].astype(BF).astype(F32)
    kn_s[...] = k.astype(BF).astype(F32)
    vn_s[...] = v.astype(BF).astype(F32)
    scale = HEAD_DIM ** -0.5
    lane_lo = lax.broadcasted_iota(jnp.int32, (1, LANES), 1) < HEAD_DIM

    def per_seq_tile(bi, carry):
        b0 = pl.multiple_of(bi * SUBLANES, SUBLANES)
        q8 = q_s[pl.ds(b0, SUBLANES), :]
        kn8 = kn_s[pl.ds(b0, SUBLANES), :]
        vn8 = vn_s[pl.ds(b0, SUBLANES), :]
        rows = []
        for u in range(SUBLANES):
            kb = ck_ref[b0 + u].astype(F32)
            vb = cv_ref[b0 + u].astype(F32)
            kn = kn8[u:u + 1]
            vnw = vn8[u:u + 1]
            pieces = []
            for r in range(GQ):
                qp = q8[u:u + 1, r * LANES:(r + 1) * LANES]
                s = _half_sums(kb * qp, lane_lo) * scale + bsamp_ref[r]
                sn = _half_sums(kn * qp, lane_lo) * scale + bnew_ref[r]
                sink = sinkrow_ref[r]
                mx = jnp.maximum(jnp.maximum(jnp.max(s, axis=0, keepdims=True), sn), sink)
                e = jnp.exp(s - mx)
                en = jnp.exp(sn - mx)
                den = jnp.sum(e, axis=0, keepdims=True) + en + jnp.exp(sink - mx)
                p = (e / den).astype(BF).astype(F32)
                pn = (en / den).astype(BF).astype(F32)
                pieces.append(jnp.sum(p * vb, axis=0, keepdims=True) + pn * vnw)
            rows.append(jnp.concatenate(pieces, axis=1))
        yb_s[pl.ds(b0, SUBLANES), :] = jnp.concatenate(rows, axis=0)
        return carry

    lax.fori_loop(0, n // SUBLANES, per_seq_tile, 0)
    yb = yb_s[...]

    hc = z[:, 1536:1792] * z[:, 1792:2048]
    hc_ref[...] = hc
    cw = convw_ref[...]
    yc = z[:, 1280:1536] * (s0_ref[...] * cw[0:1] + s1_ref[...] * cw[1:2] + hc * cw[2:3])
    _post_mixers(x, ya, yb, yc, mg_ref, wout_ref, ln1g_ref, ln1b_ref, wrt_ref, rb_ref, swgu_ref, swd_ref,
                 base_ref, xp_ref, gw_ref)


def _sample_mixer(first_layer, xin, base, xp, gw, p, ck, cv, s0, s1):
    n = DEC_BATCH
    tail = T_PROMPT // n
    anyspec = pl.BlockSpec(memory_space=pl.ANY)
    if first_layer:
        xin_spec = pl.BlockSpec((n, D_MODEL), lambda i: (0, 0))
        lead, lead_specs, aliases = (base, xp, gw, xin), [anyspec] * 3, {0: 0, 1: 1, 2: 2}
    else:
        xin_spec = pl.BlockSpec((n * ROWS_F32, LANES), lambda i: (tail, 0))
        lead, lead_specs, aliases = (xp, gw, base), [anyspec] * 2, {2: 0, 0: 1, 1: 2}
    in_specs = lead_specs + [
        xin_spec, _full((1, D_MODEL)), _full((1, D_MODEL)),
        _full((D_MODEL, IN_COLS)), _full((D_MODEL, D_MODEL)), _full((1, W_A)), _full((1, W_A)),
        _full((1, W_A)), _full((1, W_A)),
        _full((n, WINDOW, LANES)), _full((n, WINDOW, LANES)),
        _full((GQ, WINDOW, LANES)), _full((GQ, 1, LANES)), _full((GQ, 1, LANES)),
        _full((n, W_C)), _full((n, W_C)),
        _full((CONV_W, W_C)), _full((1, D_MODEL)), _full((1, D_MODEL)), _full((1, D_MODEL)),
        _full((N_EXPERTS, D_MODEL)), _full((N_EXPERTS, 1)), _full((D_MODEL, 2 * D_EXPERT)),
        _full((D_EXPERT, D_MODEL)),
    ]
    out_shape = [
        jax.ShapeDtypeStruct(base.shape, base.dtype),
        jax.ShapeDtypeStruct(xp.shape, xp.dtype),
        jax.ShapeDtypeStruct(gw.shape, gw.dtype),
        jax.ShapeDtypeStruct((n, LANES), F32),
        jax.ShapeDtypeStruct((n, LANES), F32),
        jax.ShapeDtypeStruct((n, W_C), F32),
        jax.ShapeDtypeStruct((n, W_A), F32),
    ]
    out_specs = [
        pl.BlockSpec((n * ROWS_F32, LANES), lambda i: (tail, 0)),
        pl.BlockSpec((n * ROWS_PK, LANES), lambda i: (tail, 0)),
        pl.BlockSpec((N_EXPERTS, n), lambda i: (0, tail)),
        _full((n, LANES)), _full((n, LANES)), _full((n, W_C)), _full((n, W_A)),
    ]
    return pl.pallas_call(
        functools.partial(_sample_mixer_kernel, first_layer),
        grid=(1,), in_specs=in_specs, out_specs=out_specs, out_shape=out_shape,
        scratch_shapes=[pltpu.VMEM((n, W_B), F32), pltpu.VMEM((n, W_B), F32),
                        pltpu.VMEM((n, LANES), F32), pltpu.VMEM((n, LANES), F32)],
        input_output_aliases=aliases,
        compiler_params=pltpu.CompilerParams(dimension_semantics=("arbitrary",),
                                             vmem_limit_bytes=VMEM_LIMIT),
        name="sample_mixer",
    )(*lead, p["ln_prev_g"], p["ln_prev_b"], p["w_in"], p["w_out"], p["sgu_w0"], p["sgu_b0"],
      p["ln_v_g"], p["ln_v_b"], ck, cv, p["bias_samp"], p["bias_new"], p["sink_row"], s0, s1,
      p["conv_w"], p["merge_g"], p["ln1_g"], p["ln1_b"], p["router_wt"], p["router_b"], p["sw_gu"],
      p["sw_d"])


LIST_LEN = -(-(TG * TOP_K + RB) // 1024) * 1024
UNROLL = 8


def _expert_kernel(off_ref, tok_ref, gate_ref, xp_ref, wg_ref, wu_ref, wdn_ref, base_any, out_any,
                   acc, xs, xbf, ybuf, wgu_s, wd_s, sem):
    g = pl.program_id(0)
    e = pl.program_id(1)
    rows = TG * ROWS_F32

    def group_copy(to_vmem):
        hbm = (base_any if to_vmem else out_any).at[pl.ds(g * rows, rows)]
        vm = acc.at[pl.ds(0, rows)]
        return pltpu.make_async_copy(hbm, vm, sem) if to_vmem else pltpu.make_async_copy(vm, hbm, sem)

    @pl.when(e == 0)
    def _():
        cp = group_copy(True)
        cp.start()
        acc[pl.ds(rows, ROWS_F32), :] = jnp.zeros((ROWS_F32, LANES), F32)
        xs[...] = jnp.zeros_like(xs)
        cp.wait()

    wgu_s[:, :D_EXPERT] = wg_ref[0, 0].astype(BF)
    wgu_s[:, D_EXPERT:] = wu_ref[0, 0].astype(BF)
    wd_s[...] = wdn_ref[0, 0].astype(BF)

    start = off_ref[g * (N_EXPERTS + 1) + e]
    end = off_ref[g * (N_EXPERTS + 1) + e + 1]
    nblocks = (end - start + RB - 1) // RB

    def block(bi, carry):
        r0 = start + bi * RB
        nrows = jnp.minimum(end - r0, RB)
        ntrips = (nrows + UNROLL - 1) // UNROLL

        def gather(i, c):
            for u in range(UNROLL):
                r = i * UNROLL + u
                src = pl.multiple_of(tok_ref[r0 + r] * ROWS_PK, ROWS_PK)
                dst = pl.multiple_of(r * ROWS_PK, ROWS_PK)
                xs[pl.ds(dst, ROWS_PK), :] = xp_ref[pl.ds(src, ROWS_PK), :]
            return c

        lax.fori_loop(0, ntrips, gather, 0)
        half = D_MODEL // 2

        def run_sub_blocks(n_sub):
            for lo in range(0, n_sub * SUB, SUB):
                sub_block(lo)

        def sub_block(lo):
            for j in range(ROWS_PK):
                u32 = xs[pl.ds(lo * ROWS_PK + j, SUB, stride=ROWS_PK), :]
                xbf[lo:lo + SUB, j * LANES:(j + 1) * LANES] = pltpu.bitcast(u32 << 16, F32).astype(BF)
                xbf[lo:lo + SUB, half + j * LANES:half + (j + 1) * LANES] = pltpu.bitcast(
                    u32 & jnp.uint32(0xFFFF0000), F32).astype(BF)
            hg = _dot(xbf[lo:lo + SUB, :], wgu_s[...])
            hid = jax.nn.silu(hg[:, :D_EXPERT]) * hg[:, D_EXPERT:]
            y = _dot(hid.astype(BF), wd_s[...])
            for c in range(ROWS_F32):
                ybuf[pl.ds(lo * ROWS_F32 + c, SUB, stride=ROWS_F32), :] = y[:, c * LANES:(c + 1) * LANES]

        lax.cond(nrows > SUB, lambda: run_sub_blocks(RB // SUB), lambda: run_sub_blocks(1))

        def scatter(checked, i, c):
            dsts, vals = [], []
            for u in range(UNROLL):
                r = i * UNROLL + u
                tk, gt = tok_ref[r0 + r], gate_ref[r0 + r]
                if checked:
                    valid = r < nrows
                    tk, gt = jnp.where(valid, tk, TG), jnp.where(valid, gt, 0.0)
                dst = pl.multiple_of(tk * ROWS_F32, ROWS_F32)
                src = pl.multiple_of(r * ROWS_F32, ROWS_F32)
                dsts.append(dst)
                vals.append(acc[pl.ds(dst, ROWS_F32), :] + gt * ybuf[pl.ds(src, ROWS_F32), :])
            for dst, val in zip(dsts, vals):
                acc[pl.ds(dst, ROWS_F32), :] = val
            return c

        nfull = nrows // UNROLL
        lax.fori_loop(0, nfull, functools.partial(scatter, False), 0)
        lax.fori_loop(nfull, ntrips, functools.partial(scatter, True), 0)
        return carry

    lax.fori_loop(0, nblocks, block, 0)

    @pl.when(e == N_EXPERTS - 1)
    def _():
        cp = group_copy(False)
        cp.start()
        cp.wait()


def _experts(layer, off, tok, gate, xp, wg, wu, wdn, base):
    lists = pl.BlockSpec((LIST_LEN,), lambda g, e, *_: (g,), memory_space=pltpu.SMEM)
    anyspec = pl.BlockSpec(memory_space=pl.ANY)
    in_specs = [
        lists, lists,
        pl.BlockSpec((TG * ROWS_PK, LANES), lambda g, e, *_: (g, 0)),
        pl.BlockSpec((1, 1, D_MODEL, D_EXPERT), lambda g, e, *_: (layer, e, 0, 0)),
        pl.BlockSpec((1, 1, D_MODEL, D_EXPERT), lambda g, e, *_: (layer, e, 0, 0)),
        pl.BlockSpec((1, 1, D_EXPERT, D_MODEL), lambda g, e, *_: (layer, e, 0, 0)),
        anyspec,
    ]
    return pl.pallas_call(
        _expert_kernel,
        grid_spec=pltpu.PrefetchScalarGridSpec(
            num_scalar_prefetch=1, grid=(N_TGROUPS, N_EXPERTS), in_specs=in_specs,
            out_specs=anyspec,
            scratch_shapes=[pltpu.VMEM(((TG + 1) * ROWS_F32, LANES), F32),
                            pltpu.VMEM((RB * ROWS_PK, LANES), jnp.uint32),
                            pltpu.VMEM((RB, D_MODEL), BF),
                            pltpu.VMEM((RB * ROWS_F32, LANES), F32),
                            pltpu.VMEM((D_MODEL, 2 * D_EXPERT), BF),
                            pltpu.VMEM((D_EXPERT, D_MODEL), BF),
                            pltpu.SemaphoreType.DMA]),
        out_shape=jax.ShapeDtypeStruct(base.shape, base.dtype),
        input_output_aliases={7: 0},
        compiler_params=pltpu.CompilerParams(dimension_semantics=("arbitrary", "arbitrary"),
                                             vmem_limit_bytes=VMEM_LIMIT),
        name="experts",
    )(off, tok, gate, xp, wg, wu, wdn, base)


def _dispatch_lists(gw):
    chosen = (gw > 0).reshape(N_EXPERTS, N_TGROUPS, TG).transpose(1, 0, 2)
    counts = jnp.sum(chosen, axis=-1, dtype=jnp.int32)
    off = jnp.concatenate([jnp.zeros((N_TGROUPS, 1), jnp.int32), jnp.cumsum(counts, axis=1)], axis=1)
    flat = chosen.reshape(N_TGROUPS, N_EXPERTS * TG)
    idx = jax.vmap(lambda f: jnp.nonzero(f, size=TG * TOP_K, fill_value=0)[0])(flat).astype(jnp.int32)
    gflat = gw.reshape(N_EXPERTS, N_TGROUPS, TG).transpose(1, 0, 2).reshape(N_TGROUPS, N_EXPERTS * TG)
    gate = jnp.take_along_axis(gflat, idx, axis=1)
    pad = ((0, 0), (0, LIST_LEN - TG * TOP_K))
    tok = jnp.pad(idx % TG, pad).reshape(-1)
    return off.reshape(-1), tok, jnp.pad(gate, pad).reshape(-1)


def _final_ln_kernel(n, s_ref, g_ref, b_ref, o_ref):
    o_ref[...] = _load_tokens(False, s_ref, n, g_ref, b_ref)


def _final_ln(s2, g, b, n, steps, first_block):
    return pl.pallas_call(
        functools.partial(_final_ln_kernel, n),
        grid=(steps,),
        in_specs=[pl.BlockSpec((n * ROWS_F32, LANES), lambda i: (first_block + i, 0)),
                  _full((1, D_MODEL)), _full((1, D_MODEL))],
        out_specs=pl.BlockSpec((n, D_MODEL), lambda i: (i, 0)),
        out_shape=jax.ShapeDtypeStruct((n * steps, D_MODEL), F32),
        compiler_params=pltpu.CompilerParams(dimension_semantics=("arbitrary",)),
        name="final_norm",
    )(s2, g, b)


_HEAD_AT = [(p % 2) * GQ + p // 2 for p in range(N_HEADS)]


def _bucket(dist):
    n = np.maximum(dist, 0)
    nf = np.maximum(n, 1).astype(np.float32)
    large = MAX_EXACT + (np.log(nf / np.float32(MAX_EXACT)) / np.float32(math.log(MAX_DIST / MAX_EXACT))
                         * np.float32(N_BUCKETS - MAX_EXACT)).astype(np.int32)
    return np.where(n < MAX_EXACT, n, np.minimum(large, N_BUCKETS - 1)).astype(np.int32)


def _bias_tables(rel_bias):
    qi = np.arange(WINDOW)[:, None]
    kk = np.arange(WINDOW)[None, :]

    def table(dist):
        valid = (dist >= 0) & (dist < WINDOW)
        t = jnp.where(valid[..., None], rel_bias[_bucket(dist)], NEG)
        return jnp.moveaxis(t, -1, 0).reshape(KV_HEADS, GQ * WINDOW, WINDOW)

    prev = table(WINDOW + qi - kk)
    bias_prev = jnp.stack([prev, jnp.full_like(prev, NEG)])
    bias_cur = table(qi - kk)
    dist_s = WINDOW - np.arange(WINDOW)
    col = jnp.where(((dist_s >= 0) & (dist_s < WINDOW))[:, None], rel_bias[_bucket(dist_s)], NEG)
    pairs = [jnp.concatenate([jnp.broadcast_to(col[:, r:r + 1], (WINDOW, HEAD_DIM)),
                              jnp.broadcast_to(col[:, GQ + r:GQ + r + 1], (WINDOW, HEAD_DIM))], axis=1)
             for r in range(GQ)]
    new = rel_bias[0]
    bias_new = jnp.stack([_pair_row(new, r) for r in range(GQ)])
    return bias_prev, bias_cur, jnp.stack(pairs), bias_new


def _pair_row(per_head, r):
    return jnp.concatenate([jnp.broadcast_to(per_head[r], (1, HEAD_DIM)),
                            jnp.broadcast_to(per_head[GQ + r], (1, HEAD_DIM))], axis=1)


def _prep_layer(l, tables, w_in, w_out, ln_v_g, ln_v_b, sgu_w, sgu_b, attn_sinks, conv_w, merge_g,
                ln1_g, ln1_b, ln2_g, ln2_b, router_w, router_bias, sw_gate, sw_up, sw_down):
    q0 = 2 * W_A
    qcols = np.concatenate([q0 + h * HEAD_DIM + np.arange(HEAD_DIM) for h in _HEAD_AT])
    cols = np.concatenate([np.arange(q0), qcols, np.arange(q0 + W_B, IN_COLS)])
    rows = np.concatenate([np.arange(W_A), qcols - q0 + W_A, np.arange(W_A + W_B, D_MODEL)])
    row = lambda a: a.reshape(1, -1)
    prev = max(l - 1, 0)
    bias_prev, bias_cur, bias_samp, bias_new = tables
    return dict(
        sinks=attn_sinks[l],
        ln_prev_g=row(ln2_g[prev]), ln_prev_b=row(ln2_b[prev]),
        w_in=w_in[l][:, cols].astype(BF), w_out=w_out[l][rows].astype(BF),
        sgu_w=jnp.tril(sgu_w[l]).reshape(H_A * CHUNK, CHUNK).astype(BF),
        sgu_b=jnp.repeat(sgu_b[l].T, HD_A, axis=1),
        sgu_w0=row(jnp.repeat(sgu_w[l][:, 0, 0].astype(BF).astype(F32), HD_A)),
        sgu_b0=row(jnp.repeat(sgu_b[l][:, 0], HD_A)),
        ln_v_g=row(ln_v_g[l]), ln_v_b=row(ln_v_b[l]),
        bias_prev=bias_prev, bias_cur=bias_cur, bias_samp=bias_samp, bias_new=bias_new,
        sink_row=jnp.stack([_pair_row(attn_sinks[l], r) for r in range(GQ)]),
        conv_w=conv_w[l].T, merge_g=row(merge_g[l][rows]), ln1_g=row(ln1_g[l]), ln1_b=row(ln1_b[l]),
        router_wt=router_w[l].T.astype(BF), router_b=router_bias[l].reshape(N_EXPERTS, 1),
        sw_gu=jnp.concatenate([sw_gate[l], sw_up[l]], axis=1).astype(BF), sw_d=sw_down[l].astype(BF),
    )


def kernel(x_prompt, x_sample, cache_k, cache_v, state_conv, rel_bias, w_in, w_out, ln_v_g, ln_v_b,
           sgu_w, sgu_b, attn_sinks, conv_w, merge_g, ln1_g, ln1_b, ln2_g, ln2_b, router_w, router_bias,
           ew_gate, ew_up, ew_down, sw_gate, sw_up, sw_down):
    tables = _bias_tables(rel_bias)
    xin_p = x_prompt.reshape(T_PROMPT, D_MODEL)
    xin_s = x_sample.reshape(DEC_BATCH, D_MODEL)
    kv_shape = (WINDOW, KV_HEADS, HEAD_DIM)
    outs = [[] for _ in range(8)]
    for l in range(DEPTH):
        p = _prep_layer(l, tables, w_in, w_out, ln_v_g, ln_v_b, sgu_w, sgu_b, attn_sinks, conv_w, merge_g,
                        ln1_g, ln1_b, ln2_g, ln2_b, router_w, router_bias, sw_gate, sw_up, sw_down)
        if l == 0:
            donors = (jnp.zeros((T_ALL * ROWS_F32, LANES), F32), jnp.zeros((T_ALL * ROWS_PK, LANES), jnp.uint32),
                      jnp.zeros((N_EXPERTS, T_ALL), F32))
        else:
            donors = (xp, gw)
        base, xp, gw, klast, vlast, ctail, vrows = _prompt_mixer(l == 0, xin_p, p, donors)
        ck = cache_k[l].astype(BF).reshape(DEC_BATCH, WINDOW, LANES)
        cv = cache_v[l].astype(BF).reshape(DEC_BATCH, WINDOW, LANES)
        s0, s1 = state_conv[l][:, 0], state_conv[l][:, 1]
        base, xp, gw, knew, vnew, hc, vns = _sample_mixer(l == 0, xin_s, base, xp, gw, p, ck, cv, s0, s1)
        off, tok, gate = _dispatch_lists(gw)
        s2 = _experts(l, off, tok, gate, xp, ew_gate, ew_up, ew_down, base)
        xin_p = xin_s = s2
        outs[0].append(klast.reshape((BATCH,) + kv_shape))
        outs[1].append(vlast.reshape((BATCH,) + kv_shape))
        outs[2].append(ctail)
        outs[3].append(vrows)
        outs[4].append(jnp.concatenate([cache_k[l][:, 1:], knew.reshape((DEC_BATCH, 1) + kv_shape[1:])], axis=1))
        outs[5].append(jnp.concatenate([cache_v[l][:, 1:], vnew.reshape((DEC_BATCH, 1) + kv_shape[1:])], axis=1))
        outs[6].append(jnp.stack([s1, hc], axis=1))
        outs[7].append(vns[:, None])
    g2, b2 = ln2_g[DEPTH - 1].reshape(1, -1), ln2_b[DEPTH - 1].reshape(1, -1)
    y_prompt = _final_ln(s2, g2, b2, TM, T_PROMPT // TM, 0).reshape(BATCH, SEQ, D_MODEL)
    y_sample = _final_ln(s2, g2, b2, DEC_BATCH, 1, T_PROMPT // DEC_BATCH).reshape(DEC_BATCH, 1, D_MODEL)
    return (y_prompt, y_sample) + tuple(jnp.stack(o) for o in outs)
```

```python
import functools
import math

import numpy as np
import jax
import jax.numpy as jnp
from jax import lax
from jax.experimental import pallas as pl
from jax.experimental.pallas import tpu as pltpu

D_MODEL = 1024
BATCH = 2
SEQ = 8192
DEPTH = 2
DEC_BATCH = 128
PAST_LEN = 16384
W_A = 256
H_A = 4
HD_A = 64
CHUNK = 128
W_B = 512
HEAD_DIM = 64
N_HEADS = 8
KV_HEADS = 2
GQ = 4
WINDOW = 128
W_C = 256
CONV_W = 3
N_BUCKETS = 32
MAX_EXACT = 16
MAX_DIST = 128
N_EXPERTS = 64
TOP_K = 8
N_GROUPS = 8
TOPK_GROUPS = 4
GROUP_SIZE = N_EXPERTS // N_GROUPS
D_EXPERT = 256
ROUTE_SCALE = 2.5
ALPHA = (2 * DEPTH) ** 0.25
LN_EPS = 1e-5
IN_COLS = 2048

T_PROMPT = BATCH * SEQ
T_ALL = T_PROMPT + DEC_BATCH

LANES = 128
SUBLANES = 8
VMEM_LIMIT = 56 * 1024 * 1024

TM = 512
NBLK = TM // WINDOW
TILES_PER_SEQ = SEQ // TM
ROWS_F32 = D_MODEL // LANES
ROWS_PK = ROWS_F32 // 2
N_TGROUPS = 4
TG = T_ALL // N_TGROUPS
RB = 256
SUB = 128
NEG = -1e30

BF = jnp.bfloat16
F32 = jnp.float32


def _ln(x, g, b):
    mu = jnp.mean(x, -1, keepdims=True)
    xc = x - mu
    var = jnp.mean(xc * xc, -1, keepdims=True)
    return xc * lax.rsqrt(var + LN_EPS) * g + b


def _rms(x, g):
    return x * lax.rsqrt(jnp.mean(x * x, -1, keepdims=True) + LN_EPS) * g


def _dot(a, b):
    return jnp.dot(a, b, preferred_element_type=F32)


def _dot_nt(a, b):
    return lax.dot_general(a, b, (((1,), (1,)), ((), ())), preferred_element_type=F32)


def _pack_pairs(lo, hi):
    lo_u = pltpu.bitcast(lo.astype(BF).astype(F32), jnp.uint32)
    hi_u = pltpu.bitcast(hi.astype(BF).astype(F32), jnp.uint32)
    return (hi_u & jnp.uint32(0xFFFF0000)) | (lo_u >> 16)


def _route(logits_t, rbias):
    n = logits_t.shape[1]
    scores = jax.nn.sigmoid(logits_t)
    biased = scores + rbias
    g3 = biased.reshape(N_GROUPS, GROUP_SIZE, n)
    sub = lax.broadcasted_iota(jnp.int32, g3.shape, 1)
    m1 = jnp.max(g3, axis=1, keepdims=True)
    first = jnp.min(jnp.where(g3 == m1, sub, GROUP_SIZE), axis=1, keepdims=True)
    m2 = jnp.max(jnp.where(sub == first, -jnp.inf, g3), axis=1, keepdims=True)
    gs = (m1 + m2).reshape(N_GROUPS, n)
    gi = lax.broadcasted_iota(jnp.int32, gs.shape, 0)
    grank = jnp.zeros(gs.shape, jnp.int32)
    for j in range(N_GROUPS):
        row = gs[j:j + 1]
        grank = grank + ((row > gs) | ((row == gs) & (j < gi))).astype(jnp.int32)
    gsel = (grank < TOPK_GROUPS).reshape(N_GROUPS, 1, n)
    masked = jnp.where(gsel, g3, -jnp.inf).reshape(N_EXPERTS, n)
    ei = lax.broadcasted_iota(jnp.int32, masked.shape, 0)
    rank = jnp.zeros(masked.shape, jnp.int32)
    for j in range(N_EXPERTS):
        row = masked[j:j + 1]
        rank = rank + ((row > masked) | ((row == masked) & (j < ei))).astype(jnp.int32)
    sel = rank < TOP_K
    w = jnp.where(sel, scores, 0.0)
    return w / jnp.sum(w, axis=0, keepdims=True) * ROUTE_SCALE


def _post_mixers(x, ya, yb, yc, mg_ref, wout_ref, ln1g_ref, ln1b_ref, wrt_ref, rb_ref,
                 swgu_ref, swd_ref, base_ref, xp_ref, gw_ref):
    n = x.shape[0]
    mg = mg_ref[...]
    cat = jnp.concatenate([_rms(ya, mg[:, :W_A]), _rms(yb, mg[:, W_A:W_A + W_B]),
                           _rms(yc, mg[:, W_A + W_B:])], axis=-1)
    m = _dot(cat.astype(BF), wout_ref[...])
    x1 = _ln(ALPHA * x + m, ln1g_ref[...], ln1b_ref[...])
    x1b = x1.astype(BF)
    gw_ref[...] = _route(_dot_nt(wrt_ref[...], x1b), rb_ref[...])
    hg = _dot(x1b, swgu_ref[...])
    hs = jax.nn.silu(hg[:, :D_EXPERT]) * hg[:, D_EXPERT:]
    base = ALPHA * x1 + _dot(hs.astype(BF), swd_ref[...])
    for c in range(ROWS_F32):
        base_ref[pl.ds(c, n, stride=ROWS_F32), :] = base[:, c * LANES:(c + 1) * LANES]
    half = D_MODEL // 2
    for j in range(ROWS_PK):
        xp_ref[pl.ds(j, n, stride=ROWS_PK), :] = _pack_pairs(
            x1[:, j * LANES:(j + 1) * LANES], x1[:, half + j * LANES:half + (j + 1) * LANES])


def _load_tokens(first_layer, xin_ref, n, lng_ref, lnb_ref):
    if first_layer:
        return xin_ref[...]
    s = jnp.concatenate([xin_ref[pl.ds(c, n, stride=ROWS_F32), :] for c in range(ROWS_F32)], axis=-1)
    return _ln(s, lng_ref[...], lnb_ref[...])


def _head_mask_sum(r):
    lane = lax.broadcasted_iota(jnp.int32, (CHUNK, W_A), 1) // HD_A
    out = r[0:CHUNK]
    for h in range(1, H_A):
        out = jnp.where(lane == h, r[h * CHUNK:(h + 1) * CHUNK], out)
    return out


def _prompt_mixer_kernel(first_layer, sinks_ref, *refs):
    (xin_ref, lng_ref, lnb_ref, win_ref, wout_ref, sguw_ref,
     sgub_ref, lnvg_ref, lnvb_ref, bprev_ref, bcur_ref, convw_ref, mg_ref, ln1g_ref,
     ln1b_ref, wrt_ref, rb_ref, swgu_ref, swd_ref,
     base_ref, xp_ref, gw_ref, klast_ref, vlast_ref, ctail_ref, vrows_ref,
     kprev_s, vprev_s, hcprev_s) = refs[(3 if first_layer else 2):]
    t = pl.program_id(1)
    is_first = t == 0

    @pl.when(is_first)
    def _():
        kprev_s[...] = jnp.zeros_like(kprev_s)
        vprev_s[...] = jnp.zeros_like(vprev_s)
        hcprev_s[...] = jnp.zeros_like(hcprev_s)

    x = _load_tokens(first_layer, xin_ref, TM, lng_ref, lnb_ref)
    z = _dot(x.astype(BF), win_ref[...])
    ua = jax.nn.gelu(z[:, 0:W_A])
    vn = _ln(jax.nn.gelu(z[:, W_A:2 * W_A]), lnvg_ref[...], lnvb_ref[...])
    q = z[:, 512:1024]
    k = z[:, 1024:1152]
    v = z[:, 1152:1280]
    gate_b = z[:, 1280:1536]
    gate_c = z[:, 1536:1792]
    h = z[:, 1792:2048]

    vnb = vn.astype(BF)
    sgub = sgub_ref[...]
    mixed = [_head_mask_sum(_dot(sguw_ref[...], vnb[c * CHUNK:(c + 1) * CHUNK])) + sgub for c in range(NBLK)]
    ya = ua * jnp.concatenate(mixed, axis=0)

    kb = k.astype(BF)
    vb = v.astype(BF)
    qb = q.astype(BF)
    scale = HEAD_DIM ** -0.5
    yb_blocks = []
    for c in range(NBLK):
        if c == 0:
            kp, vp = kprev_s[...], vprev_s[...]
            bsel = jnp.where(is_first, 1, 0)
        else:
            kp, vp = kb[(c - 1) * WINDOW:c * WINDOW], vb[(c - 1) * WINDOW:c * WINDOW]
            bsel = 0
        kc, vc = kb[c * WINDOW:(c + 1) * WINDOW], vb[c * WINDOW:(c + 1) * WINDOW]
        qc = qb[c * WINDOW:(c + 1) * WINDOW]
        outs = [None] * N_HEADS
        for g in range(KV_HEADS):
            gs = slice(g * HEAD_DIM, (g + 1) * HEAD_DIM)
            qg = jnp.concatenate([qc[:, (2 * r + g) * HEAD_DIM:(2 * r + g + 1) * HEAD_DIM] for r in range(GQ)], axis=0)
            sp = _dot_nt(qg, kp[:, gs]) * scale + bprev_ref[bsel, g]
            sc = _dot_nt(qg, kc[:, gs]) * scale + bcur_ref[g]
            for r in range(GQ):
                rs = slice(r * WINDOW, (r + 1) * WINDOW)
                sink = sinks_ref[g * GQ + r]
                spr, scr = sp[rs], sc[rs]
                mx = jnp.maximum(jnp.maximum(jnp.max(spr, -1, keepdims=True), jnp.max(scr, -1, keepdims=True)), sink)
                ep, ec = jnp.exp(spr - mx), jnp.exp(scr - mx)
                den = jnp.sum(ep, -1, keepdims=True) + jnp.sum(ec, -1, keepdims=True) + jnp.exp(sink - mx)
                o = _dot((ep / den).astype(BF), vp[:, gs]) + _dot((ec / den).astype(BF), vc[:, gs])
                outs[2 * r + g] = o
        yb_blocks.append(jnp.concatenate(outs, axis=-1))
    yb = jnp.concatenate(yb_blocks, axis=0)
    kprev_s[...] = kb[TM - WINDOW:]
    vprev_s[...] = vb[TM - WINDOW:]

    hc = gate_c * h
    prev = hcprev_s[...]
    row = lax.broadcasted_iota(jnp.int32, hc.shape, 0)
    h1 = jnp.where(row == 0, prev[7:8], pltpu.roll(hc, 1, 0))
    h2 = jnp.where(row == 0, prev[6:7], jnp.where(row == 1, prev[7:8], pltpu.roll(hc, 2, 0)))
    cw = convw_ref[...]
    yc = gate_b * (h2 * cw[0:1] + h1 * cw[1:2] + hc * cw[2:3])
    hcprev_s[...] = hc[TM - SUBLANES:]

    @pl.when(t == TILES_PER_SEQ - 1)
    def _():
        klast_ref[0] = k[TM - WINDOW:]
        vlast_ref[0] = v[TM - WINDOW:]
        ctail_ref[0] = hc[TM - (CONV_W - 1):]
        vrows_ref[0] = vn[TM - CHUNK:]

    _post_mixers(x, ya, yb, yc, mg_ref, wout_ref, ln1g_ref, ln1b_ref, wrt_ref, rb_ref, swgu_ref, swd_ref,
                 base_ref, xp_ref, gw_ref)


def _full(shape):
    nd = len(shape)
    return pl.BlockSpec(shape, lambda *_: (0,) * nd)


def _prompt_mixer(first_layer, xin, p, donors):
    nt = TILES_PER_SEQ
    tok = lambda b, t, *_: (b * nt + t, 0)
    anyspec = pl.BlockSpec(memory_space=pl.ANY)
    if first_layer:
        xin_spec = pl.BlockSpec((TM, D_MODEL), tok)
        aliases = {1: 0, 2: 1, 3: 2}
    else:
        xin_spec = pl.BlockSpec((TM * ROWS_F32, LANES), tok)
        aliases = {3: 0, 1: 1, 2: 2}
    in_specs = [anyspec] * len(donors) + [
        xin_spec, _full((1, D_MODEL)), _full((1, D_MODEL)),
        _full((D_MODEL, IN_COLS)), _full((D_MODEL, D_MODEL)), _full((H_A * CHUNK, CHUNK)),
        _full((CHUNK, W_A)), _full((1, W_A)), _full((1, W_A)),
        _full((2, KV_HEADS, GQ * WINDOW, WINDOW)), _full((KV_HEADS, GQ * WINDOW, WINDOW)),
        _full((CONV_W, W_C)), _full((1, D_MODEL)), _full((1, D_MODEL)), _full((1, D_MODEL)),
        _full((N_EXPERTS, D_MODEL)), _full((N_EXPERTS, 1)), _full((D_MODEL, 2 * D_EXPERT)),
        _full((D_EXPERT, D_MODEL)),
    ]
    per_seq = lambda b, t, *_: (b, 0, 0)
    out_shape = [
        jax.ShapeDtypeStruct((T_ALL * ROWS_F32, LANES), F32),
        jax.ShapeDtypeStruct((T_ALL * ROWS_PK, LANES), jnp.uint32),
        jax.ShapeDtypeStruct((N_EXPERTS, T_ALL), F32),
        jax.ShapeDtypeStruct((BATCH, WINDOW, KV_HEADS * HEAD_DIM), F32),
        jax.ShapeDtypeStruct((BATCH, WINDOW, KV_HEADS * HEAD_DIM), F32),
        jax.ShapeDtypeStruct((BATCH, CONV_W - 1, W_C), F32),
        jax.ShapeDtypeStruct((BATCH, CHUNK, W_A), F32),
    ]
    out_specs = [
        pl.BlockSpec((TM * ROWS_F32, LANES), tok),
        pl.BlockSpec((TM * ROWS_PK, LANES), tok),
        pl.BlockSpec((N_EXPERTS, TM), lambda b, t, *_: (0, b * nt + t)),
        pl.BlockSpec((1, WINDOW, KV_HEADS * HEAD_DIM), per_seq),
        pl.BlockSpec((1, WINDOW, KV_HEADS * HEAD_DIM), per_seq),
        pl.BlockSpec((1, CONV_W - 1, W_C), per_seq),
        pl.BlockSpec((1, CHUNK, W_A), per_seq),
    ]
    return pl.pallas_call(
        functools.partial(_prompt_mixer_kernel, first_layer),
        grid_spec=pltpu.PrefetchScalarGridSpec(
            num_scalar_prefetch=1, grid=(BATCH, nt), in_specs=in_specs, out_specs=out_specs,
            scratch_shapes=[pltpu.VMEM((WINDOW, KV_HEADS * HEAD_DIM), BF),
                            pltpu.VMEM((WINDOW, KV_HEADS * HEAD_DIM), BF),
                            pltpu.VMEM((SUBLANES, W_C), F32)]),
        out_shape=out_shape,
        input_output_aliases=aliases,
        compiler_params=pltpu.CompilerParams(dimension_semantics=("arbitrary", "arbitrary"),
                                             vmem_limit_bytes=VMEM_LIMIT),
        name="prompt_mixer",
    )(p["sinks"], *donors, xin, p["ln_prev_g"], p["ln_prev_b"], p["w_in"], p["w_out"], p["sgu_w"], p["sgu_b"],
      p["ln_v_g"], p["ln_v_b"], p["bias_prev"], p["bias_cur"], p["conv_w"], p["merge_g"], p["ln1_g"],
      p["ln1_b"], p["router_wt"], p["router_b"], p["sw_gu"], p["sw_d"])


def _half_sums(a, lane_lo):
    s0 = jnp.sum(jnp.where(lane_lo, a, 0.0), axis=1, keepdims=True)
    s1 = jnp.sum(jnp.where(lane_lo, 0.0, a), axis=1, keepdims=True)
    return jnp.where(lane_lo, s0, s1)


def _sample_mixer_kernel(first_layer, *refs):
    (xin_ref, lng_ref, lnb_ref, win_ref, wout_ref,
     sguw0_ref, sgub0_ref, lnvg_ref, lnvb_ref, ck_ref, cv_ref, bsamp_ref, bnew_ref,
     sinkrow_ref, s0_ref, s1_ref, convw_ref, mg_ref, ln1g_ref, ln1b_ref, wrt_ref,
     rb_ref, swgu_ref, swd_ref,
     base_ref, xp_ref, gw_ref, knew_ref, vnew_ref, hc_ref, vn_ref,
     q_s, yb_s, kn_s, vn_s) = refs[(3 if first_layer else 2):]
    n = DEC_BATCH
    x = _load_tokens(first_layer, xin_ref, n, lng_ref, lnb_ref)
    z = _dot(x.astype(BF), win_ref[...])
    ua = jax.nn.gelu(z[:, 0:W_A])
    vn = _ln(jax.nn.gelu(z[:, W_A:2 * W_A]), lnvg_ref[...], lnvb_ref[...])
    vn_ref[...] = vn
    ya = ua * (vn.astype(BF).astype(F32) * sguw0_ref[...] + sgub0_ref[...])
    k = z[:, 1024:1152]
    v = z[:, 1152:1280]
    knew_ref[...] = k
    vnew_ref[...] = v
    q_s[...] = z[:, 512:1024].astype(BF).astype(F32)
    kn_s[...] = k.astype(BF).astype(F32)
    vn_s[...] = v.astype(BF).astype(F32)
    scale = HEAD_DIM ** -0.5
    lane_lo = lax.broadcasted_iota(jnp.int32, (1, LANES), 1) < HEAD_DIM

    def per_seq_tile(bi, carry):
        b0 = pl.multiple_of(bi * SUBLANES, SUBLANES)
        q8 = q_s[pl.ds(b0, SUBLANES), :]
        kn8 = kn_s[pl.ds(b0, SUBLANES), :]
        vn8 = vn_s[pl.ds(b0, SUBLANES), :]
        rows = []
        for u in range(SUBLANES):
            kb = ck_ref[b0 + u].astype(F32)
            vb = cv_ref[b0 + u].astype(F32)
            kn = kn8[u:u + 1]
            vnw = vn8[u:u + 1]
            pieces = []
            for r in range(GQ):
                qp = q8[u:u + 1, r * LANES:(r + 1) * LANES]
                s = _half_sums(kb * qp, lane_lo) * scale + bsamp_ref[r]
                sn = _half_sums(kn * qp, lane_lo) * scale + bnew_ref[r]
                sink = sinkrow_ref[r]
                mx = jnp.maximum(jnp.maximum(jnp.max(s, axis=0, keepdims=True), sn), sink)
                e = jnp.exp(s - mx)
                en = jnp.exp(sn - mx)
                den = jnp.sum(e, axis=0, keepdims=True) + en + jnp.exp(sink - mx)
                p = (e / den).astype(BF).astype(F32)
                pn = (en / den).astype(BF).astype(F32)
                pieces.append(jnp.sum(p * vb, axis=0, keepdims=True) + pn * vnw)
            rows.append(jnp.concatenate(pieces, axis=1))
        yb_s[pl.ds(b0, SUBLANES), :] = jnp.concatenate(rows, axis=0)
        return carry

    lax.fori_loop(0, n // SUBLANES, per_seq_tile, 0)
    yb = yb_s[...]

    hc = z[:, 1536:1792] * z[:, 1792:2048]
    hc_ref[...] = hc
    cw = convw_ref[...]
    yc = z[:, 1280:1536] * (s0_ref[...] * cw[0:1] + s1_ref[...] * cw[1:2] + hc * cw[2:3])
    _post_mixers(x, ya, yb, yc, mg_ref, wout_ref, ln1g_ref, ln1b_ref, wrt_ref, rb_ref, swgu_ref, swd_ref,
                 base_ref, xp_ref, gw_ref)


def _sample_mixer(first_layer, xin, base, xp, gw, p, ck, cv, s0, s1):
    n = DEC_BATCH
    tail = T_PROMPT // n
    anyspec = pl.BlockSpec(memory_space=pl.ANY)
    if first_layer:
        xin_spec = pl.BlockSpec((n, D_MODEL), lambda i: (0, 0))
        lead, lead_specs, aliases = (base, xp, gw, xin), [anyspec] * 3, {0: 0, 1: 1, 2: 2}
    else:
        xin_spec = pl.BlockSpec((n * ROWS_F32, LANES), lambda i: (tail, 0))
        lead, lead_specs, aliases = (xp, gw, base), [anyspec] * 2, {2: 0, 0: 1, 1: 2}
    in_specs = lead_specs + [
        xin_spec, _full((1, D_MODEL)), _full((1, D_MODEL)),
        _full((D_MODEL, IN_COLS)), _full((D_MODEL, D_MODEL)), _full((1, W_A)), _full((1, W_A)),
        _full((1, W_A)), _full((1, W_A)),
        _full((n, WINDOW, LANES)), _full((n, WINDOW, LANES)),
        _full((GQ, WINDOW, LANES)), _full((GQ, 1, LANES)), _full((GQ, 1, LANES)),
        _full((n, W_C)), _full((n, W_C)),
        _full((CONV_W, W_C)), _full((1, D_MODEL)), _full((1, D_MODEL)), _full((1, D_MODEL)),
        _full((N_EXPERTS, D_MODEL)), _full((N_EXPERTS, 1)), _full((D_MODEL, 2 * D_EXPERT)),
        _full((D_EXPERT, D_MODEL)),
    ]
    out_shape = [
        jax.ShapeDtypeStruct(base.shape, base.dtype),
        jax.ShapeDtypeStruct(xp.shape, xp.dtype),
        jax.ShapeDtypeStruct(gw.shape, gw.dtype),
        jax.ShapeDtypeStruct((n, LANES), F32),
        jax.ShapeDtypeStruct((n, LANES), F32),
        jax.ShapeDtypeStruct((n, W_C), F32),
        jax.ShapeDtypeStruct((n, W_A), F32),
    ]
    out_specs = [
        pl.BlockSpec((n * ROWS_F32, LANES), lambda i: (tail, 0)),
        pl.BlockSpec((n * ROWS_PK, LANES), lambda i: (tail, 0)),
        pl.BlockSpec((N_EXPERTS, n), lambda i: (0, tail)),
        _full((n, LANES)), _full((n, LANES)), _full((n, W_C)), _full((n, W_A)),
    ]
    return pl.pallas_call(
        functools.partial(_sample_mixer_kernel, first_layer),
        grid=(1,), in_specs=in_specs, out_specs=out_specs, out_shape=out_shape,
        scratch_shapes=[pltpu.VMEM((n, W_B), F32), pltpu.VMEM((n, W_B), F32),
                        pltpu.VMEM((n, LANES), F32), pltpu.VMEM((n, LANES), F32)],
        input_output_aliases=aliases,
        compiler_params=pltpu.CompilerParams(dimension_semantics=("arbitrary",),
                                             vmem_limit_bytes=VMEM_LIMIT),
        name="sample_mixer",
    )(*lead, p["ln_prev_g"], p["ln_prev_b"], p["w_in"], p["w_out"], p["sgu_w0"], p["sgu_b0"],
      p["ln_v_g"], p["ln_v_b"], ck, cv, p["bias_samp"], p["bias_new"], p["sink_row"], s0, s1,
      p["conv_w"], p["merge_g"], p["ln1_g"], p["ln1_b"], p["router_wt"], p["router_b"], p["sw_gu"],
      p["sw_d"])


LIST_LEN = -(-(TG * TOP_K + RB) // 1024) * 1024
UNROLL = 8
GATHER_UNROLL = 16


def _expert_kernel(off_ref, tok_ref, gate_ref, xp_ref, wg_ref, wu_ref, wdn_ref, base_any, out_any,
                   acc, xs, xbf, ybuf, wgu_s, wd_s, sem):
    g = pl.program_id(0)
    e = pl.program_id(1)
    rows = TG * ROWS_F32

    def group_copy(to_vmem):
        hbm = (base_any if to_vmem else out_any).at[pl.ds(g * rows, rows)]
        vm = acc.at[pl.ds(0, rows)]
        return pltpu.make_async_copy(hbm, vm, sem) if to_vmem else pltpu.make_async_copy(vm, hbm, sem)

    @pl.when(e == 0)
    def _():
        cp = group_copy(True)
        cp.start()
        acc[pl.ds(rows, ROWS_F32), :] = jnp.zeros((ROWS_F32, LANES), F32)
        xs[...] = jnp.zeros_like(xs)
        cp.wait()

    wgu_s[:, :D_EXPERT] = wg_ref[0, 0].astype(BF)
    wgu_s[:, D_EXPERT:] = wu_ref[0, 0].astype(BF)
    wd_s[...] = wdn_ref[0, 0].astype(BF)

    start = off_ref[g * (N_EXPERTS + 1) + e]
    end = off_ref[g * (N_EXPERTS + 1) + e + 1]
    nblocks = (end - start + RB - 1) // RB

    def block(bi, carry):
        r0 = start + bi * RB
        nrows = jnp.minimum(end - r0, RB)
        ntrips = (nrows + UNROLL - 1) // UNROLL

        def gather(i, c):
            for u in range(GATHER_UNROLL):
                r = i * GATHER_UNROLL + u
                src = pl.multiple_of(tok_ref[r0 + r] * ROWS_PK, ROWS_PK)
                dst = pl.multiple_of(r * ROWS_PK, ROWS_PK)
                xs[pl.ds(dst, ROWS_PK), :] = xp_ref[pl.ds(src, ROWS_PK), :]
            return c

        lax.fori_loop(0, (nrows + GATHER_UNROLL - 1) // GATHER_UNROLL, gather, 0)
        half = D_MODEL // 2

        def run_sub_blocks(n_sub):
            for lo in range(0, n_sub * SUB, SUB):
                sub_block(lo)

        def sub_block(lo):
            for j in range(ROWS_PK):
                u32 = xs[pl.ds(lo * ROWS_PK + j, SUB, stride=ROWS_PK), :]
                xbf[lo:lo + SUB, j * LANES:(j + 1) * LANES] = pltpu.bitcast(u32 << 16, F32).astype(BF)
                xbf[lo:lo + SUB, half + j * LANES:half + (j + 1) * LANES] = pltpu.bitcast(
                    u32 & jnp.uint32(0xFFFF0000), F32).astype(BF)
            hg = _dot(xbf[lo:lo + SUB, :], wgu_s[...])
            hid = jax.nn.silu(hg[:, :D_EXPERT]) * hg[:, D_EXPERT:]
            y = _dot(hid.astype(BF), wd_s[...])
            for c in range(ROWS_F32):
                ybuf[pl.ds(lo * ROWS_F32 + c, SUB, stride=ROWS_F32), :] = y[:, c * LANES:(c + 1) * LANES]

        lax.cond(nrows > SUB, lambda: run_sub_blocks(RB // SUB), lambda: run_sub_blocks(1))

        def scatter(checked, i, c):
            dsts, vals = [], []
            for u in range(UNROLL):
                r = i * UNROLL + u
                tk, gt = tok_ref[r0 + r], gate_ref[r0 + r]
                if checked:
                    valid = r < nrows
                    tk, gt = jnp.where(valid, tk, TG), jnp.where(valid, gt, 0.0)
                dst = pl.multiple_of(tk * ROWS_F32, ROWS_F32)
                src = pl.multiple_of(r * ROWS_F32, ROWS_F32)
                dsts.append(dst)
                vals.append(acc[pl.ds(dst, ROWS_F32), :] + gt * ybuf[pl.ds(src, ROWS_F32), :])
            for dst, val in zip(dsts, vals):
                acc[pl.ds(dst, ROWS_F32), :] = val
            return c

        nfull = nrows // UNROLL
        lax.fori_loop(0, nfull, functools.partial(scatter, False), 0)
        lax.fori_loop(nfull, ntrips, functools.partial(scatter, True), 0)
        return carry

    lax.fori_loop(0, nblocks, block, 0)

    @pl.when(e == N_EXPERTS - 1)
    def _():
        cp = group_copy(False)
        cp.start()
        cp.wait()


def _experts(layer, off, tok, gate, xp, wg, wu, wdn, base):
    lists = pl.BlockSpec((LIST_LEN,), lambda g, e, *_: (g,), memory_space=pltpu.SMEM)
    anyspec = pl.BlockSpec(memory_space=pl.ANY)
    in_specs = [
        lists, lists,
        pl.BlockSpec((TG * ROWS_PK, LANES), lambda g, e, *_: (g, 0)),
        pl.BlockSpec((1, 1, D_MODEL, D_EXPERT), lambda g, e, *_: (layer, e, 0, 0)),
        pl.BlockSpec((1, 1, D_MODEL, D_EXPERT), lambda g, e, *_: (layer, e, 0, 0)),
        pl.BlockSpec((1, 1, D_EXPERT, D_MODEL), lambda g, e, *_: (layer, e, 0, 0)),
        anyspec,
    ]
    return pl.pallas_call(
        _expert_kernel,
        grid_spec=pltpu.PrefetchScalarGridSpec(
            num_scalar_prefetch=1, grid=(N_TGROUPS, N_EXPERTS), in_specs=in_specs,
            out_specs=anyspec,
            scratch_shapes=[pltpu.VMEM(((TG + 1) * ROWS_F32, LANES), F32),
                            pltpu.VMEM((RB * ROWS_PK, LANES), jnp.uint32),
                            pltpu.VMEM((RB, D_MODEL), BF),
                            pltpu.VMEM((RB * ROWS_F32, LANES), F32),
                            pltpu.VMEM((D_MODEL, 2 * D_EXPERT), BF),
                            pltpu.VMEM((D_EXPERT, D_MODEL), BF),
                            pltpu.SemaphoreType.DMA]),
        out_shape=jax.ShapeDtypeStruct(base.shape, base.dtype),
        input_output_aliases={7: 0},
        compiler_params=pltpu.CompilerParams(dimension_semantics=("arbitrary", "arbitrary"),
                                             vmem_limit_bytes=VMEM_LIMIT),
        name="experts",
    )(off, tok, gate, xp, wg, wu, wdn, base)


def _dispatch_lists(gw):
    chosen = (gw > 0).reshape(N_EXPERTS, N_TGROUPS, TG).transpose(1, 0, 2)
    counts = jnp.sum(chosen, axis=-1, dtype=jnp.int32)
    off = jnp.concatenate([jnp.zeros((N_TGROUPS, 1), jnp.int32), jnp.cumsum(counts, axis=1)], axis=1)
    flat = chosen.reshape(N_TGROUPS, N_EXPERTS * TG)
    idx = jax.vmap(lambda f: jnp.nonzero(f, size=TG * TOP_K, fill_value=0)[0])(flat).astype(jnp.int32)
    gflat = gw.reshape(N_EXPERTS, N_TGROUPS, TG).transpose(1, 0, 2).reshape(N_TGROUPS, N_EXPERTS * TG)
    gate = jnp.take_along_axis(gflat, idx, axis=1)
    pad = ((0, 0), (0, LIST_LEN - TG * TOP_K))
    tok = jnp.pad(idx % TG, pad).reshape(-1)
    return off.reshape(-1), tok, jnp.pad(gate, pad).reshape(-1)


def _final_ln_kernel(n, s_ref, g_ref, b_ref, o_ref):
    o_ref[...] = _load_tokens(False, s_ref, n, g_ref, b_ref)


def _final_ln(s2, g, b, n, steps, first_block):
    return pl.pallas_call(
        functools.partial(_final_ln_kernel, n),
        grid=(steps,),
        in_specs=[pl.BlockSpec((n * ROWS_F32, LANES), lambda i: (first_block + i, 0)),
                  _full((1, D_MODEL)), _full((1, D_MODEL))],
        out_specs=pl.BlockSpec((n, D_MODEL), lambda i: (i, 0)),
        out_shape=jax.ShapeDtypeStruct((n * steps, D_MODEL), F32),
        compiler_params=pltpu.CompilerParams(dimension_semantics=("arbitrary",)),
        name="final_norm",
    )(s2, g, b)


_HEAD_AT = [(p % 2) * GQ + p // 2 for p in range(N_HEADS)]


def _bucket(dist):
    n = np.maximum(dist, 0)
    nf = np.maximum(n, 1).astype(np.float32)
    large = MAX_EXACT + (np.log(nf / np.float32(MAX_EXACT)) / np.float32(math.log(MAX_DIST / MAX_EXACT))
                         * np.float32(N_BUCKETS - MAX_EXACT)).astype(np.int32)
    return np.where(n < MAX_EXACT, n, np.minimum(large, N_BUCKETS - 1)).astype(np.int32)


def _bias_tables(rel_bias):
    qi = np.arange(WINDOW)[:, None]
    kk = np.arange(WINDOW)[None, :]

    def table(dist):
        valid = (dist >= 0) & (dist < WINDOW)
        t = jnp.where(valid[..., None], rel_bias[_bucket(dist)], NEG)
        return jnp.moveaxis(t, -1, 0).reshape(KV_HEADS, GQ * WINDOW, WINDOW)

    prev = table(WINDOW + qi - kk)
    bias_prev = jnp.stack([prev, jnp.full_like(prev, NEG)])
    bias_cur = table(qi - kk)
    dist_s = WINDOW - np.arange(WINDOW)
    col = jnp.where(((dist_s >= 0) & (dist_s < WINDOW))[:, None], rel_bias[_bucket(dist_s)], NEG)
    pairs = [jnp.concatenate([jnp.broadcast_to(col[:, r:r + 1], (WINDOW, HEAD_DIM)),
                              jnp.broadcast_to(col[:, GQ + r:GQ + r + 1], (WINDOW, HEAD_DIM))], axis=1)
             for r in range(GQ)]
    new = rel_bias[0]
    bias_new = jnp.stack([_pair_row(new, r) for r in range(GQ)])
    return bias_prev, bias_cur, jnp.stack(pairs), bias_new


def _pair_row(per_head, r):
    return jnp.concatenate([jnp.broadcast_to(per_head[r], (1, HEAD_DIM)),
                            jnp.broadcast_to(per_head[GQ + r], (1, HEAD_DIM))], axis=1)


def _prep_layer(l, tables, w_in, w_out, ln_v_g, ln_v_b, sgu_w, sgu_b, attn_sinks, conv_w, merge_g,
                ln1_g, ln1_b, ln2_g, ln2_b, router_w, router_bias, sw_gate, sw_up, sw_down):
    q0 = 2 * W_A
    qcols = np.concatenate([q0 + h * HEAD_DIM + np.arange(HEAD_DIM) for h in _HEAD_AT])
    cols = np.concatenate([np.arange(q0), qcols, np.arange(q0 + W_B, IN_COLS)])
    rows = np.concatenate([np.arange(W_A), qcols - q0 + W_A, np.arange(W_A + W_B, D_MODEL)])
    row = lambda a: a.reshape(1, -1)
    prev = max(l - 1, 0)
    bias_prev, bias_cur, bias_samp, bias_new = tables
    return dict(
        sinks=attn_sinks[l],
        ln_prev_g=row(ln2_g[prev]), ln_prev_b=row(ln2_b[prev]),
        w_in=w_in[l][:, cols].astype(BF), w_out=w_out[l][rows].astype(BF),
        sgu_w=jnp.tril(sgu_w[l]).reshape(H_A * CHUNK, CHUNK).astype(BF),
        sgu_b=jnp.repeat(sgu_b[l].T, HD_A, axis=1),
        sgu_w0=row(jnp.repeat(sgu_w[l][:, 0, 0].astype(BF).astype(F32), HD_A)),
        sgu_b0=row(jnp.repeat(sgu_b[l][:, 0], HD_A)),
        ln_v_g=row(ln_v_g[l]), ln_v_b=row(ln_v_b[l]),
        bias_prev=bias_prev, bias_cur=bias_cur, bias_samp=bias_samp, bias_new=bias_new,
        sink_row=jnp.stack([_pair_row(attn_sinks[l], r) for r in range(GQ)]),
        conv_w=conv_w[l].T, merge_g=row(merge_g[l][rows]), ln1_g=row(ln1_g[l]), ln1_b=row(ln1_b[l]),
        router_wt=router_w[l].T.astype(BF), router_b=router_bias[l].reshape(N_EXPERTS, 1),
        sw_gu=jnp.concatenate([sw_gate[l], sw_up[l]], axis=1).astype(BF), sw_d=sw_down[l].astype(BF),
    )


def kernel(x_prompt, x_sample, cache_k, cache_v, state_conv, rel_bias, w_in, w_out, ln_v_g, ln_v_b,
           sgu_w, sgu_b, attn_sinks, conv_w, merge_g, ln1_g, ln1_b, ln2_g, ln2_b, router_w, router_bias,
           ew_gate, ew_up, ew_down, sw_gate, sw_up, sw_down):
    tables = _bias_tables(rel_bias)
    xin_p = x_prompt.reshape(T_PROMPT, D_MODEL)
    xin_s = x_sample.reshape(DEC_BATCH, D_MODEL)
    kv_shape = (WINDOW, KV_HEADS, HEAD_DIM)
    outs = [[] for _ in range(8)]
    for l in range(DEPTH):
        p = _prep_layer(l, tables, w_in, w_out, ln_v_g, ln_v_b, sgu_w, sgu_b, attn_sinks, conv_w, merge_g,
                        ln1_g, ln1_b, ln2_g, ln2_b, router_w, router_bias, sw_gate, sw_up, sw_down)
        if l == 0:
            donors = (jnp.zeros((T_ALL * ROWS_F32, LANES), F32), jnp.zeros((T_ALL * ROWS_PK, LANES), jnp.uint32),
                      jnp.zeros((N_EXPERTS, T_ALL), F32))
        else:
            donors = (xp, gw)
        base, xp, gw, klast, vlast, ctail, vrows = _prompt_mixer(l == 0, xin_p, p, donors)
        ck = cache_k[l].astype(BF).reshape(DEC_BATCH, WINDOW, LANES)
        cv = cache_v[l].astype(BF).reshape(DEC_BATCH, WINDOW, LANES)
        s0, s1 = state_conv[l][:, 0], state_conv[l][:, 1]
        base, xp, gw, knew, vnew, hc, vns = _sample_mixer(l == 0, xin_s, base, xp, gw, p, ck, cv, s0, s1)
        off, tok, gate = _dispatch_lists(gw)
        s2 = _experts(l, off, tok, gate, xp, ew_gate, ew_up, ew_down, base)
        xin_p = xin_s = s2
        outs[0].append(klast.reshape((BATCH,) + kv_shape))
        outs[1].append(vlast.reshape((BATCH,) + kv_shape))
        outs[2].append(ctail)
        outs[3].append(vrows)
        outs[4].append(jnp.concatenate([cache_k[l][:, 1:], knew.reshape((DEC_BATCH, 1) + kv_shape[1:])], axis=1))
        outs[5].append(jnp.concatenate([cache_v[l][:, 1:], vnew.reshape((DEC_BATCH, 1) + kv_shape[1:])], axis=1))
        outs[6].append(jnp.stack([s1, hc], axis=1))
        outs[7].append(vns[:, None])
    g2, b2 = ln2_g[DEPTH - 1].reshape(1, -1), ln2_b[DEPTH - 1].reshape(1, -1)
    y_prompt = _final_ln(s2, g2, b2, TM, T_PROMPT // TM, 0).reshape(BATCH, SEQ, D_MODEL)
    y_sample = _final_ln(s2, g2, b2, DEC_BATCH, 1, T_PROMPT // DEC_BATCH).reshape(DEC_BATCH, 1, D_MODEL)
    return (y_prompt, y_sample) + tuple(jnp.stack(o) for o in outs)
```

```python
import functools
import math

import numpy as np
import jax
import jax.numpy as jnp
from jax import lax
from jax.experimental import pallas as pl
from jax.experimental.pallas import tpu as pltpu

D_MODEL = 1024
BATCH = 2
SEQ = 8192
DEPTH = 2
DEC_BATCH = 128
PAST_LEN = 16384
W_A = 256
H_A = 4
HD_A = 64
CHUNK = 128
W_B = 512
HEAD_DIM = 64
N_HEADS = 8
KV_HEADS = 2
GQ = 4
WINDOW = 128
W_C = 256
CONV_W = 3
N_BUCKETS = 32
MAX_EXACT = 16
MAX_DIST = 128
N_EXPERTS = 64
TOP_K = 8
N_GROUPS = 8
TOPK_GROUPS = 4
GROUP_SIZE = N_EXPERTS // N_GROUPS
D_EXPERT = 256
ROUTE_SCALE = 2.5
ALPHA = (2 * DEPTH) ** 0.25
LN_EPS = 1e-5
IN_COLS = 2048

T_PROMPT = BATCH * SEQ
T_ALL = T_PROMPT + DEC_BATCH

LANES = 128
SUBLANES = 8
VMEM_LIMIT = 56 * 1024 * 1024

TM = 512
NBLK = TM // WINDOW
TILES_PER_SEQ = SEQ // TM
ROWS_F32 = D_MODEL // LANES
ROWS_PK = ROWS_F32 // 2
N_TGROUPS = 4
TG = T_ALL // N_TGROUPS
RB = 256
SUB = 128
NEG = -1e30

BF = jnp.bfloat16
F32 = jnp.float32


def _ln(x, g, b):
    mu = jnp.mean(x, -1, keepdims=True)
    xc = x - mu
    var = jnp.mean(xc * xc, -1, keepdims=True)
    return xc * lax.rsqrt(var + LN_EPS) * g + b


def _rms(x, g):
    return x * lax.rsqrt(jnp.mean(x * x, -1, keepdims=True) + LN_EPS) * g


def _dot(a, b):
    return jnp.dot(a, b, preferred_element_type=F32)


def _dot_nt(a, b):
    return lax.dot_general(a, b, (((1,), (1,)), ((), ())), preferred_element_type=F32)


def _pack_pairs(lo, hi):
    lo_u = pltpu.bitcast(lo.astype(BF).astype(F32), jnp.uint32)
    hi_u = pltpu.bitcast(hi.astype(BF).astype(F32), jnp.uint32)
    return (hi_u & jnp.uint32(0xFFFF0000)) | (lo_u >> 16)


def _route(logits_t, rbias):
    n = logits_t.shape[1]
    scores = jax.nn.sigmoid(logits_t)
    biased = scores + rbias
    g3 = biased.reshape(N_GROUPS, GROUP_SIZE, n)
    sub = lax.broadcasted_iota(jnp.int32, g3.shape, 1)
    m1 = jnp.max(g3, axis=1, keepdims=True)
    first = jnp.min(jnp.where(g3 == m1, sub, GROUP_SIZE), axis=1, keepdims=True)
    m2 = jnp.max(jnp.where(sub == first, -jnp.inf, g3), axis=1, keepdims=True)
    gs = (m1 + m2).reshape(N_GROUPS, n)
    gi = lax.broadcasted_iota(jnp.int32, gs.shape, 0)
    grank = jnp.zeros(gs.shape, jnp.int32)
    for j in range(N_GROUPS):
        row = gs[j:j + 1]
        grank = grank + ((row > gs) | ((row == gs) & (j < gi))).astype(jnp.int32)
    gsel = (grank < TOPK_GROUPS).reshape(N_GROUPS, 1, n)
    masked = jnp.where(gsel, g3, -jnp.inf).reshape(N_EXPERTS, n)
    ei = lax.broadcasted_iota(jnp.int32, masked.shape, 0)
    rank = jnp.zeros(masked.shape, jnp.int32)
    for j in range(N_EXPERTS):
        row = masked[j:j + 1]
        rank = rank + ((row > masked) | ((row == masked) & (j < ei))).astype(jnp.int32)
    sel = rank < TOP_K
    w = jnp.where(sel, scores, 0.0)
    return w / jnp.sum(w, axis=0, keepdims=True) * ROUTE_SCALE


def _post_mixers(x, ya, yb, yc, mg_ref, wout_ref, ln1g_ref, ln1b_ref, wrt_ref, rb_ref,
                 swgu_ref, swd_ref, base_ref, xp_ref, gw_ref):
    n = x.shape[0]
    mg = mg_ref[...]
    cat = jnp.concatenate([_rms(ya, mg[:, :W_A]), _rms(yb, mg[:, W_A:W_A + W_B]),
                           _rms(yc, mg[:, W_A + W_B:])], axis=-1)
    m = _dot(cat.astype(BF), wout_ref[...])
    x1 = _ln(ALPHA * x + m, ln1g_ref[...], ln1b_ref[...])
    x1b = x1.astype(BF)
    gw_ref[...] = _route(_dot_nt(wrt_ref[...], x1b), rb_ref[...])
    hg = _dot(x1b, swgu_ref[...])
    hs = jax.nn.silu(hg[:, :D_EXPERT]) * hg[:, D_EXPERT:]
    base = ALPHA * x1 + _dot(hs.astype(BF), swd_ref[...])
    for c in range(ROWS_F32):
        base_ref[pl.ds(c, n, stride=ROWS_F32), :] = base[:, c * LANES:(c + 1) * LANES]
    half = D_MODEL // 2
    for j in range(ROWS_PK):
        xp_ref[pl.ds(j, n, stride=ROWS_PK), :] = _pack_pairs(
            x1[:, j * LANES:(j + 1) * LANES], x1[:, half + j * LANES:half + (j + 1) * LANES])


def _load_tokens(first_layer, xin_ref, n, lng_ref, lnb_ref):
    if first_layer:
        return xin_ref[...]
    s = jnp.concatenate([xin_ref[pl.ds(c, n, stride=ROWS_F32), :] for c in range(ROWS_F32)], axis=-1)
    return _ln(s, lng_ref[...], lnb_ref[...])


def _head_mask_sum(r):
    lane = lax.broadcasted_iota(jnp.int32, (CHUNK, W_A), 1) // HD_A
    out = r[0:CHUNK]
    for h in range(1, H_A):
        out = jnp.where(lane == h, r[h * CHUNK:(h + 1) * CHUNK], out)
    return out


def _prompt_mixer_kernel(first_layer, sinks_ref, *refs):
    (xin_ref, lng_ref, lnb_ref, win_ref, wout_ref, sguw_ref,
     sgub_ref, lnvg_ref, lnvb_ref, bprev_ref, bcur_ref, convw_ref, mg_ref, ln1g_ref,
     ln1b_ref, wrt_ref, rb_ref, swgu_ref, swd_ref,
     base_ref, xp_ref, gw_ref, klast_ref, vlast_ref, ctail_ref, vrows_ref,
     kprev_s, vprev_s, hcprev_s) = refs[(3 if first_layer else 2):]
    t = pl.program_id(1)
    is_first = t == 0

    @pl.when(is_first)
    def _():
        kprev_s[...] = jnp.zeros_like(kprev_s)
        vprev_s[...] = jnp.zeros_like(vprev_s)
        hcprev_s[...] = jnp.zeros_like(hcprev_s)

    x = _load_tokens(first_layer, xin_ref, TM, lng_ref, lnb_ref)
    z = _dot(x.astype(BF), win_ref[...])
    ua = jax.nn.gelu(z[:, 0:W_A])
    vn = _ln(jax.nn.gelu(z[:, W_A:2 * W_A]), lnvg_ref[...], lnvb_ref[...])
    q = z[:, 512:1024]
    k = z[:, 1024:1152]
    v = z[:, 1152:1280]
    gate_b = z[:, 1280:1536]
    gate_c = z[:, 1536:1792]
    h = z[:, 1792:2048]

    vnb = vn.astype(BF)
    sgub = sgub_ref[...]
    mixed = [_head_mask_sum(_dot(sguw_ref[...], vnb[c * CHUNK:(c + 1) * CHUNK])) + sgub for c in range(NBLK)]
    ya = ua * jnp.concatenate(mixed, axis=0)

    kb = k.astype(BF)
    vb = v.astype(BF)
    qb = q.astype(BF)
    scale = HEAD_DIM ** -0.5
    yb_blocks = []
    for c in range(NBLK):
        if c == 0:
            kp, vp = kprev_s[...], vprev_s[...]
            bsel = jnp.where(is_first, 1, 0)
        else:
            kp, vp = kb[(c - 1) * WINDOW:c * WINDOW], vb[(c - 1) * WINDOW:c * WINDOW]
            bsel = 0
        kc, vc = kb[c * WINDOW:(c + 1) * WINDOW], vb[c * WINDOW:(c + 1) * WINDOW]
        qc = qb[c * WINDOW:(c + 1) * WINDOW]
        outs = [None] * N_HEADS
        for g in range(KV_HEADS):
            gs = slice(g * HEAD_DIM, (g + 1) * HEAD_DIM)
            qg = jnp.concatenate([qc[:, (2 * r + g) * HEAD_DIM:(2 * r + g + 1) * HEAD_DIM] for r in range(GQ)], axis=0)
            sp = _dot_nt(qg, kp[:, gs]) * scale + bprev_ref[bsel, g]
            sc = _dot_nt(qg, kc[:, gs]) * scale + bcur_ref[g]
            for r in range(GQ):
                rs = slice(r * WINDOW, (r + 1) * WINDOW)
                sink = sinks_ref[g * GQ + r]
                spr, scr = sp[rs], sc[rs]
                mx = jnp.maximum(jnp.maximum(jnp.max(spr, -1, keepdims=True), jnp.max(scr, -1, keepdims=True)), sink)
                ep, ec = jnp.exp(spr - mx), jnp.exp(scr - mx)
                den = jnp.sum(ep, -1, keepdims=True) + jnp.sum(ec, -1, keepdims=True) + jnp.exp(sink - mx)
                o = _dot((ep / den).astype(BF), vp[:, gs]) + _dot((ec / den).astype(BF), vc[:, gs])
                outs[2 * r + g] = o
        yb_blocks.append(jnp.concatenate(outs, axis=-1))
    yb = jnp.concatenate(yb_blocks, axis=0)
    kprev_s[...] = kb[TM - WINDOW:]
    vprev_s[...] = vb[TM - WINDOW:]

    hc = gate_c * h
    prev = hcprev_s[...]
    row = lax.broadcasted_iota(jnp.int32, hc.shape, 0)
    h1 = jnp.where(row == 0, prev[7:8], pltpu.roll(hc, 1, 0))
    h2 = jnp.where(row == 0, prev[6:7], jnp.where(row == 1, prev[7:8], pltpu.roll(hc, 2, 0)))
    cw = convw_ref[...]
    yc = gate_b * (h2 * cw[0:1] + h1 * cw[1:2] + hc * cw[2:3])
    hcprev_s[...] = hc[TM - SUBLANES:]

    @pl.when(t == TILES_PER_SEQ - 1)
    def _():
        klast_ref[0] = k[TM - WINDOW:]
        vlast_ref[0] = v[TM - WINDOW:]
        ctail_ref[0] = hc[TM - (CONV_W - 1):]
        vrows_ref[0] = vn[TM - CHUNK:]

    _post_mixers(x, ya, yb, yc, mg_ref, wout_ref, ln1g_ref, ln1b_ref, wrt_ref, rb_ref, swgu_ref, swd_ref,
                 base_ref, xp_ref, gw_ref)


def _full(shape):
    nd = len(shape)
    return pl.BlockSpec(shape, lambda *_: (0,) * nd)


def _prompt_mixer(first_layer, xin, p, donors):
    nt = TILES_PER_SEQ
    tok = lambda b, t, *_: (b * nt + t, 0)
    anyspec = pl.BlockSpec(memory_space=pl.ANY)
    if first_layer:
        xin_spec = pl.BlockSpec((TM, D_MODEL), tok)
        aliases = {1: 0, 2: 1, 3: 2}
    else:
        xin_spec = pl.BlockSpec((TM * ROWS_F32, LANES), tok)
        aliases = {3: 0, 1: 1, 2: 2}
    in_specs = [anyspec] * len(donors) + [
        xin_spec, _full((1, D_MODEL)), _full((1, D_MODEL)),
        _full((D_MODEL, IN_COLS)), _full((D_MODEL, D_MODEL)), _full((H_A * CHUNK, CHUNK)),
        _full((CHUNK, W_A)), _full((1, W_A)), _full((1, W_A)),
        _full((2, KV_HEADS, GQ * WINDOW, WINDOW)), _full((KV_HEADS, GQ * WINDOW, WINDOW)),
        _full((CONV_W, W_C)), _full((1, D_MODEL)), _full((1, D_MODEL)), _full((1, D_MODEL)),
        _full((N_EXPERTS, D_MODEL)), _full((N_EXPERTS, 1)), _full((D_MODEL, 2 * D_EXPERT)),
        _full((D_EXPERT, D_MODEL)),
    ]
    per_seq = lambda b, t, *_: (b, 0, 0)
    out_shape = [
        jax.ShapeDtypeStruct((T_ALL * ROWS_F32, LANES), F32),
        jax.ShapeDtypeStruct((T_ALL * ROWS_PK, LANES), jnp.uint32),
        jax.ShapeDtypeStruct((N_EXPERTS, T_ALL), F32),
        jax.ShapeDtypeStruct((BATCH, WINDOW, KV_HEADS * HEAD_DIM), F32),
        jax.ShapeDtypeStruct((BATCH, WINDOW, KV_HEADS * HEAD_DIM), F32),
        jax.ShapeDtypeStruct((BATCH, CONV_W - 1, W_C), F32),
        jax.ShapeDtypeStruct((BATCH, CHUNK, W_A), F32),
    ]
    out_specs = [
        pl.BlockSpec((TM * ROWS_F32, LANES), tok),
        pl.BlockSpec((TM * ROWS_PK, LANES), tok),
        pl.BlockSpec((N_EXPERTS, TM), lambda b, t, *_: (0, b * nt + t)),
        pl.BlockSpec((1, WINDOW, KV_HEADS * HEAD_DIM), per_seq),
        pl.BlockSpec((1, WINDOW, KV_HEADS * HEAD_DIM), per_seq),
        pl.BlockSpec((1, CONV_W - 1, W_C), per_seq),
        pl.BlockSpec((1, CHUNK, W_A), per_seq),
    ]
    return pl.pallas_call(
        functools.partial(_prompt_mixer_kernel, first_layer),
        grid_spec=pltpu.PrefetchScalarGridSpec(
            num_scalar_prefetch=1, grid=(BATCH, nt), in_specs=in_specs, out_specs=out_specs,
            scratch_shapes=[pltpu.VMEM((WINDOW, KV_HEADS * HEAD_DIM), BF),
                            pltpu.VMEM((WINDOW, KV_HEADS * HEAD_DIM), BF),
                            pltpu.VMEM((SUBLANES, W_C), F32)]),
        out_shape=out_shape,
        input_output_aliases=aliases,
        compiler_params=pltpu.CompilerParams(dimension_semantics=("arbitrary", "arbitrary"),
                                             vmem_limit_bytes=VMEM_LIMIT),
        name="prompt_mixer",
    )(p["sinks"], *donors, xin, p["ln_prev_g"], p["ln_prev_b"], p["w_in"], p["w_out"], p["sgu_w"], p["sgu_b"],
      p["ln_v_g"], p["ln_v_b"], p["bias_prev"], p["bias_cur"], p["conv_w"], p["merge_g"], p["ln1_g"],
      p["ln1_b"], p["router_wt"], p["router_b"], p["sw_gu"], p["sw_d"])


def _half_sums(a, lane_lo):
    s0 = jnp.sum(jnp.where(lane_lo, a, 0.0), axis=1, keepdims=True)
    s1 = jnp.sum(jnp.where(lane_lo, 0.0, a), axis=1, keepdims=True)
    return jnp.where(lane_lo, s0, s1)


def _sample_mixer_kernel(first_layer, *refs):
    (xin_ref, lng_ref, lnb_ref, win_ref, wout_ref,
     sguw0_ref, sgub0_ref, lnvg_ref, lnvb_ref, ck_ref, cv_ref, bsamp_ref, bnew_ref,
     sinkrow_ref, s0_ref, s1_ref, convw_ref, mg_ref, ln1g_ref, ln1b_ref, wrt_ref,
     rb_ref, swgu_ref, swd_ref,
     base_ref, xp_ref, gw_ref, knew_ref, vnew_ref, hc_ref, vn_ref,
     q_s, yb_s, kn_s, vn_s) = refs[(3 if first_layer else 2):]
    n = DEC_BATCH
    x = _load_tokens(first_layer, xin_ref, n, lng_ref, lnb_ref)
    z = _dot(x.astype(BF), win_ref[...])
    ua = jax.nn.gelu(z[:, 0:W_A])
    vn = _ln(jax.nn.gelu(z[:, W_A:2 * W_A]), lnvg_ref[...], lnvb_ref[...])
    vn_ref[...] = vn
    ya = ua * (vn.astype(BF).astype(F32) * sguw0_ref[...] + sgub0_ref[...])
    k = z[:, 1024:1152]
    v = z[:, 1152:1280]
    knew_ref[...] = k
    vnew_ref[...] = v
    q_s[...] = z[:, 512:1024].astype(BF).astype(F32)
    kn_s[...] = k.astype(BF).astype(F32)
    vn_s[...] = v.astype(BF).astype(F32)
    scale = HEAD_DIM ** -0.5
    lane_lo = lax.broadcasted_iota(jnp.int32, (1, LANES), 1) < HEAD_DIM

    def per_seq_tile(bi, carry):
        b0 = pl.multiple_of(bi * SUBLANES, SUBLANES)
        q8 = q_s[pl.ds(b0, SUBLANES), :]
        kn8 = kn_s[pl.ds(b0, SUBLANES), :]
        vn8 = vn_s[pl.ds(b0, SUBLANES), :]
        rows = []
        for u in range(SUBLANES):
            kb = ck_ref[b0 + u].astype(F32)
            vb = cv_ref[b0 + u].astype(F32)
            kn = kn8[u:u + 1]
            vnw = vn8[u:u + 1]
            pieces = []
            for r in range(GQ):
                qp = q8[u:u + 1, r * LANES:(r + 1) * LANES]
                s = _half_sums(kb * qp, lane_lo) * scale + bsamp_ref[r]
                sn = _half_sums(kn * qp, lane_lo) * scale + bnew_ref[r]
                sink = sinkrow_ref[r]
                mx = jnp.maximum(jnp.maximum(jnp.max(s, axis=0, keepdims=True), sn), sink)
                e = jnp.exp(s - mx)
                en = jnp.exp(sn - mx)
                den = jnp.sum(e, axis=0, keepdims=True) + en + jnp.exp(sink - mx)
                p = (e / den).astype(BF).astype(F32)
                pn = (en / den).astype(BF).astype(F32)
                pieces.append(jnp.sum(p * vb, axis=0, keepdims=True) + pn * vnw)
            rows.append(jnp.concatenate(pieces, axis=1))
        yb_s[pl.ds(b0, SUBLANES), :] = jnp.concatenate(rows, axis=0)
        return carry

    lax.fori_loop(0, n // SUBLANES, per_seq_tile, 0)
    yb = yb_s[...]

    hc = z[:, 1536:1792] * z[:, 1792:2048]
    hc_ref[...] = hc
    cw = convw_ref[...]
    yc = z[:, 1280:1536] * (s0_ref[...] * cw[0:1] + s1_ref[...] * cw[1:2] + hc * cw[2:3])
    _post_mixers(x, ya, yb, yc, mg_ref, wout_ref, ln1g_ref, ln1b_ref, wrt_ref, rb_ref, swgu_ref, swd_ref,
                 base_ref, xp_ref, gw_ref)


def _sample_mixer(first_layer, xin, base, xp, gw, p, ck, cv, s0, s1):
    n = DEC_BATCH
    tail = T_PROMPT // n
    anyspec = pl.BlockSpec(memory_space=pl.ANY)
    if first_layer:
        xin_spec = pl.BlockSpec((n, D_MODEL), lambda i: (0, 0))
        lead, lead_specs, aliases = (base, xp, gw, xin), [anyspec] * 3, {0: 0, 1: 1, 2: 2}
    else:
        xin_spec = pl.BlockSpec((n * ROWS_F32, LANES), lambda i: (tail, 0))
        lead, lead_specs, aliases = (xp, gw, base), [anyspec] * 2, {2: 0, 0: 1, 1: 2}
    in_specs = lead_specs + [
        xin_spec, _full((1, D_MODEL)), _full((1, D_MODEL)),
        _full((D_MODEL, IN_COLS)), _full((D_MODEL, D_MODEL)), _full((1, W_A)), _full((1, W_A)),
        _full((1, W_A)), _full((1, W_A)),
        _full((n, WINDOW, LANES)), _full((n, WINDOW, LANES)),
        _full((GQ, WINDOW, LANES)), _full((GQ, 1, LANES)), _full((GQ, 1, LANES)),
        _full((n, W_C)), _full((n, W_C)),
        _full((CONV_W, W_C)), _full((1, D_MODEL)), _full((1, D_MODEL)), _full((1, D_MODEL)),
        _full((N_EXPERTS, D_MODEL)), _full((N_EXPERTS, 1)), _full((D_MODEL, 2 * D_EXPERT)),
        _full((D_EXPERT, D_MODEL)),
    ]
    out_shape = [
        jax.ShapeDtypeStruct(base.shape, base.dtype),
        jax.ShapeDtypeStruct(xp.shape, xp.dtype),
        jax.ShapeDtypeStruct(gw.shape, gw.dtype),
        jax.ShapeDtypeStruct((n, LANES), F32),
        jax.ShapeDtypeStruct((n, LANES), F32),
        jax.ShapeDtypeStruct((n, W_C), F32),
        jax.ShapeDtypeStruct((n, W_A), F32),
    ]
    out_specs = [
        pl.BlockSpec((n * ROWS_F32, LANES), lambda i: (tail, 0)),
        pl.BlockSpec((n * ROWS_PK, LANES), lambda i: (tail, 0)),
        pl.BlockSpec((N_EXPERTS, n), lambda i: (0, tail)),
        _full((n, LANES)), _full((n, LANES)), _full((n, W_C)), _full((n, W_A)),
    ]
    return pl.pallas_call(
        functools.partial(_sample_mixer_kernel, first_layer),
        grid=(1,), in_specs=in_specs, out_specs=out_specs, out_shape=out_shape,
        scratch_shapes=[pltpu.VMEM((n, W_B), F32), pltpu.VMEM((n, W_B), F32),
                        pltpu.VMEM((n, LANES), F32), pltpu.VMEM((n, LANES), F32)],
        input_output_aliases=aliases,
        compiler_params=pltpu.CompilerParams(dimension_semantics=("arbitrary",),
                                             vmem_limit_bytes=VMEM_LIMIT),
        name="sample_mixer",
    )(*lead, p["ln_prev_g"], p["ln_prev_b"], p["w_in"], p["w_out"], p["sgu_w0"], p["sgu_b0"],
      p["ln_v_g"], p["ln_v_b"], ck, cv, p["bias_samp"], p["bias_new"], p["sink_row"], s0, s1,
      p["conv_w"], p["merge_g"], p["ln1_g"], p["ln1_b"], p["router_wt"], p["router_b"], p["sw_gu"],
      p["sw_d"])


LIST_LEN = -(-(TG * TOP_K + RB) // 1024) * 1024
UNROLL = 8
GATHER_UNROLL = 16
INLINE_TRIPS = 16


def _expert_kernel(off_ref, tok_ref, gate_ref, xp_ref, wg_ref, wu_ref, wdn_ref, base_any, out_any,
                   acc, xs, xbf, ybuf, wgu_s, wd_s, sem):
    g = pl.program_id(0)
    e = pl.program_id(1)
    rows = TG * ROWS_F32

    def group_copy(to_vmem):
        hbm = (base_any if to_vmem else out_any).at[pl.ds(g * rows, rows)]
        vm = acc.at[pl.ds(0, rows)]
        return pltpu.make_async_copy(hbm, vm, sem) if to_vmem else pltpu.make_async_copy(vm, hbm, sem)

    @pl.when(e == 0)
    def _():
        cp = group_copy(True)
        cp.start()
        acc[pl.ds(rows, ROWS_F32), :] = jnp.zeros((ROWS_F32, LANES), F32)
        xs[...] = jnp.zeros_like(xs)
        cp.wait()

    wgu_s[:, :D_EXPERT] = wg_ref[0, 0].astype(BF)
    wgu_s[:, D_EXPERT:] = wu_ref[0, 0].astype(BF)
    wd_s[...] = wdn_ref[0, 0].astype(BF)

    start = off_ref[g * (N_EXPERTS + 1) + e]
    end = off_ref[g * (N_EXPERTS + 1) + e + 1]
    nblocks = (end - start + RB - 1) // RB

    def block(bi, carry):
        r0 = start + bi * RB
        nrows = jnp.minimum(end - r0, RB)
        ntrips = (nrows + UNROLL - 1) // UNROLL

        def gather(i, c):
            for u in range(GATHER_UNROLL):
                r = i * GATHER_UNROLL + u
                src = pl.multiple_of(tok_ref[r0 + r] * ROWS_PK, ROWS_PK)
                dst = pl.multiple_of(r * ROWS_PK, ROWS_PK)
                xs[pl.ds(dst, ROWS_PK), :] = xp_ref[pl.ds(src, ROWS_PK), :]
            return c

        lax.fori_loop(0, (nrows + GATHER_UNROLL - 1) // GATHER_UNROLL, gather, 0)
        half = D_MODEL // 2

        def sub_block(lo):
            for j in range(ROWS_PK):
                u32 = xs[pl.ds(lo * ROWS_PK + j, SUB, stride=ROWS_PK), :]
                xbf[lo:lo + SUB, j * LANES:(j + 1) * LANES] = pltpu.bitcast(u32 << 16, F32).astype(BF)
                xbf[lo:lo + SUB, half + j * LANES:half + (j + 1) * LANES] = pltpu.bitcast(
                    u32 & jnp.uint32(0xFFFF0000), F32).astype(BF)
            hg = _dot(xbf[lo:lo + SUB, :], wgu_s[...])
            hid = jax.nn.silu(hg[:, :D_EXPERT]) * hg[:, D_EXPERT:]
            y = _dot(hid.astype(BF), wd_s[...])
            for c in range(ROWS_F32):
                ybuf[pl.ds(lo * ROWS_F32 + c, SUB, stride=ROWS_F32), :] = y[:, c * LANES:(c + 1) * LANES]

        def scatter(checked, i, c):
            dsts, vals = [], []
            for u in range(UNROLL):
                r = i * UNROLL + u
                tk, gt = tok_ref[r0 + r], gate_ref[r0 + r]
                if checked:
                    valid = r < nrows
                    tk, gt = jnp.where(valid, tk, TG), jnp.where(valid, gt, 0.0)
                dst = pl.multiple_of(tk * ROWS_F32, ROWS_F32)
                src = r * ROWS_F32 if isinstance(r, int) else pl.multiple_of(r * ROWS_F32, ROWS_F32)
                dsts.append(dst)
                vals.append(acc[pl.ds(dst, ROWS_F32), :] + gt * ybuf[pl.ds(src, ROWS_F32), :])
            for dst, val in zip(dsts, vals):
                acc[pl.ds(dst, ROWS_F32), :] = val
            return c

        def two_sub_blocks():
            sub_block(0)
            for i in range(INLINE_TRIPS):
                scatter(False, i, 0)
            sub_block(SUB)

        lax.cond(nrows > SUB, two_sub_blocks, lambda: sub_block(0))

        nfull = nrows // UNROLL
        first_trip = jnp.where(nrows > SUB, INLINE_TRIPS, 0)
        lax.fori_loop(first_trip, nfull, functools.partial(scatter, False), 0)
        lax.fori_loop(nfull, ntrips, functools.partial(scatter, True), 0)
        return carry

    lax.fori_loop(0, nblocks, block, 0)

    @pl.when(e == N_EXPERTS - 1)
    def _():
        cp = group_copy(False)
        cp.start()
        cp.wait()


def _experts(layer, off, tok, gate, xp, wg, wu, wdn, base):
    lists = pl.BlockSpec((LIST_LEN,), lambda g, e, *_: (g,), memory_space=pltpu.SMEM)
    anyspec = pl.BlockSpec(memory_space=pl.ANY)
    in_specs = [
        lists, lists,
        pl.BlockSpec((TG * ROWS_PK, LANES), lambda g, e, *_: (g, 0)),
        pl.BlockSpec((1, 1, D_MODEL, D_EXPERT), lambda g, e, *_: (layer, e, 0, 0)),
        pl.BlockSpec((1, 1, D_MODEL, D_EXPERT), lambda g, e, *_: (layer, e, 0, 0)),
        pl.BlockSpec((1, 1, D_EXPERT, D_MODEL), lambda g, e, *_: (layer, e, 0, 0)),
        anyspec,
    ]
    return pl.pallas_call(
        _expert_kernel,
        grid_spec=pltpu.PrefetchScalarGridSpec(
            num_scalar_prefetch=1, grid=(N_TGROUPS, N_EXPERTS), in_specs=in_specs,
            out_specs=anyspec,
            scratch_shapes=[pltpu.VMEM(((TG + 1) * ROWS_F32, LANES), F32),
                            pltpu.VMEM((RB * ROWS_PK, LANES), jnp.uint32),
                            pltpu.VMEM((RB, D_MODEL), BF),
                            pltpu.VMEM((RB * ROWS_F32, LANES), F32),
                            pltpu.VMEM((D_MODEL, 2 * D_EXPERT), BF),
                            pltpu.VMEM((D_EXPERT, D_MODEL), BF),
                            pltpu.SemaphoreType.DMA]),
        out_shape=jax.ShapeDtypeStruct(base.shape, base.dtype),
        input_output_aliases={7: 0},
        compiler_params=pltpu.CompilerParams(dimension_semantics=("arbitrary", "arbitrary"),
                                             vmem_limit_bytes=VMEM_LIMIT),
        name="experts",
    )(off, tok, gate, xp, wg, wu, wdn, base)


def _dispatch_lists(gw):
    chosen = (gw > 0).reshape(N_EXPERTS, N_TGROUPS, TG).transpose(1, 0, 2)
    counts = jnp.sum(chosen, axis=-1, dtype=jnp.int32)
    off = jnp.concatenate([jnp.zeros((N_TGROUPS, 1), jnp.int32), jnp.cumsum(counts, axis=1)], axis=1)
    flat = chosen.reshape(N_TGROUPS, N_EXPERTS * TG)
    idx = jax.vmap(lambda f: jnp.nonzero(f, size=TG * TOP_K, fill_value=0)[0])(flat).astype(jnp.int32)
    gflat = gw.reshape(N_EXPERTS, N_TGROUPS, TG).transpose(1, 0, 2).reshape(N_TGROUPS, N_EXPERTS * TG)
    gate = jnp.take_along_axis(gflat, idx, axis=1)
    pad = ((0, 0), (0, LIST_LEN - TG * TOP_K))
    tok = jnp.pad(idx % TG, pad).reshape(-1)
    return off.reshape(-1), tok, jnp.pad(gate, pad).reshape(-1)


def _final_ln_kernel(n, s_ref, g_ref, b_ref, o_ref):
    o_ref[...] = _load_tokens(False, s_ref, n, g_ref, b_ref)


def _final_ln(s2, g, b, n, steps, first_block):
    return pl.pallas_call(
        functools.partial(_final_ln_kernel, n),
        grid=(steps,),
        in_specs=[pl.BlockSpec((n * ROWS_F32, LANES), lambda i: (first_block + i, 0)),
                  _full((1, D_MODEL)), _full((1, D_MODEL))],
        out_specs=pl.BlockSpec((n, D_MODEL), lambda i: (i, 0)),
        out_shape=jax.ShapeDtypeStruct((n * steps, D_MODEL), F32),
        compiler_params=pltpu.CompilerParams(dimension_semantics=("arbitrary",)),
        name="final_norm",
    )(s2, g, b)


_HEAD_AT = [(p % 2) * GQ + p // 2 for p in range(N_HEADS)]


def _bucket(dist):
    n = np.maximum(dist, 0)
    nf = np.maximum(n, 1).astype(np.float32)
    large = MAX_EXACT + (np.log(nf / np.float32(MAX_EXACT)) / np.float32(math.log(MAX_DIST / MAX_EXACT))
                         * np.float32(N_BUCKETS - MAX_EXACT)).astype(np.int32)
    return np.where(n < MAX_EXACT, n, np.minimum(large, N_BUCKETS - 1)).astype(np.int32)


def _bias_tables(rel_bias):
    qi = np.arange(WINDOW)[:, None]
    kk = np.arange(WINDOW)[None, :]

    def table(dist):
        valid = (dist >= 0) & (dist < WINDOW)
        t = jnp.where(valid[..., None], rel_bias[_bucket(dist)], NEG)
        return jnp.moveaxis(t, -1, 0).reshape(KV_HEADS, GQ * WINDOW, WINDOW)

    prev = table(WINDOW + qi - kk)
    bias_prev = jnp.stack([prev, jnp.full_like(prev, NEG)])
    bias_cur = table(qi - kk)
    dist_s = WINDOW - np.arange(WINDOW)
    col = jnp.where(((dist_s >= 0) & (dist_s < WINDOW))[:, None], rel_bias[_bucket(dist_s)], NEG)
    pairs = [jnp.concatenate([jnp.broadcast_to(col[:, r:r + 1], (WINDOW, HEAD_DIM)),
                              jnp.broadcast_to(col[:, GQ + r:GQ + r + 1], (WINDOW, HEAD_DIM))], axis=1)
             for r in range(GQ)]
    new = rel_bias[0]
    bias_new = jnp.stack([_pair_row(new, r) for r in range(GQ)])
    return bias_prev, bias_cur, jnp.stack(pairs), bias_new


def _pair_row(per_head, r):
    return jnp.concatenate([jnp.broadcast_to(per_head[r], (1, HEAD_DIM)),
                            jnp.broadcast_to(per_head[GQ + r], (1, HEAD_DIM))], axis=1)


def _prep_layer(l, tables, w_in, w_out, ln_v_g, ln_v_b, sgu_w, sgu_b, attn_sinks, conv_w, merge_g,
                ln1_g, ln1_b, ln2_g, ln2_b, router_w, router_bias, sw_gate, sw_up, sw_down):
    q0 = 2 * W_A
    qcols = np.concatenate([q0 + h * HEAD_DIM + np.arange(HEAD_DIM) for h in _HEAD_AT])
    cols = np.concatenate([np.arange(q0), qcols, np.arange(q0 + W_B, IN_COLS)])
    rows = np.concatenate([np.arange(W_A), qcols - q0 + W_A, np.arange(W_A + W_B, D_MODEL)])
    row = lambda a: a.reshape(1, -1)
    prev = max(l - 1, 0)
    bias_prev, bias_cur, bias_samp, bias_new = tables
    return dict(
        sinks=attn_sinks[l],
        ln_prev_g=row(ln2_g[prev]), ln_prev_b=row(ln2_b[prev]),
        w_in=w_in[l][:, cols].astype(BF), w_out=w_out[l][rows].astype(BF),
        sgu_w=jnp.tril(sgu_w[l]).reshape(H_A * CHUNK, CHUNK).astype(BF),
        sgu_b=jnp.repeat(sgu_b[l].T, HD_A, axis=1),
        sgu_w0=row(jnp.repeat(sgu_w[l][:, 0, 0].astype(BF).astype(F32), HD_A)),
        sgu_b0=row(jnp.repeat(sgu_b[l][:, 0], HD_A)),
        ln_v_g=row(ln_v_g[l]), ln_v_b=row(ln_v_b[l]),
        bias_prev=bias_prev, bias_cur=bias_cur, bias_samp=bias_samp, bias_new=bias_new,
        sink_row=jnp.stack([_pair_row(attn_sinks[l], r) for r in range(GQ)]),
        conv_w=conv_w[l].T, merge_g=row(merge_g[l][rows]), ln1_g=row(ln1_g[l]), ln1_b=row(ln1_b[l]),
        router_wt=router_w[l].T.astype(BF), router_b=router_bias[l].reshape(N_EXPERTS, 1),
        sw_gu=jnp.concatenate([sw_gate[l], sw_up[l]], axis=1).astype(BF), sw_d=sw_down[l].astype(BF),
    )


def kernel(x_prompt, x_sample, cache_k, cache_v, state_conv, rel_bias, w_in, w_out, ln_v_g, ln_v_b,
           sgu_w, sgu_b, attn_sinks, conv_w, merge_g, ln1_g, ln1_b, ln2_g, ln2_b, router_w, router_bias,
           ew_gate, ew_up, ew_down, sw_gate, sw_up, sw_down):
    tables = _bias_tables(rel_bias)
    xin_p = x_prompt.reshape(T_PROMPT, D_MODEL)
    xin_s = x_sample.reshape(DEC_BATCH, D_MODEL)
    kv_shape = (WINDOW, KV_HEADS, HEAD_DIM)
    outs = [[] for _ in range(8)]
    for l in range(DEPTH):
        p = _prep_layer(l, tables, w_in, w_out, ln_v_g, ln_v_b, sgu_w, sgu_b, attn_sinks, conv_w, merge_g,
                        ln1_g, ln1_b, ln2_g, ln2_b, router_w, router_bias, sw_gate, sw_up, sw_down)
        if l == 0:
            donors = (jnp.zeros((T_ALL * ROWS_F32, LANES), F32), jnp.zeros((T_ALL * ROWS_PK, LANES), jnp.uint32),
                      jnp.zeros((N_EXPERTS, T_ALL), F32))
        else:
            donors = (xp, gw)
        base, xp, gw, klast, vlast, ctail, vrows = _prompt_mixer(l == 0, xin_p, p, donors)
        ck = cache_k[l].astype(BF).reshape(DEC_BATCH, WINDOW, LANES)
        cv = cache_v[l].astype(BF).reshape(DEC_BATCH, WINDOW, LANES)
        s0, s1 = state_conv[l][:, 0], state_conv[l][:, 1]
        base, xp, gw, knew, vnew, hc, vns = _sample_mixer(l == 0, xin_s, base, xp, gw, p, ck, cv, s0, s1)
        off, tok, gate = _dispatch_lists(gw)
        s2 = _experts(l, off, tok, gate, xp, ew_gate, ew_up, ew_down, base)
        xin_p = xin_s = s2
        outs[0].append(klast.reshape((BATCH,) + kv_shape))
        outs[1].append(vlast.reshape((BATCH,) + kv_shape))
        outs[2].append(ctail)
        outs[3].append(vrows)
        outs[4].append(jnp.concatenate([cache_k[l][:, 1:], knew.reshape((DEC_BATCH, 1) + kv_shape[1:])], axis=1))
        outs[5].append(jnp.concatenate([cache_v[l][:, 1:], vnew.reshape((DEC_BATCH, 1) + kv_shape[1:])], axis=1))
        outs[6].append(jnp.stack([s1, hc], axis=1))
        outs[7].append(vns[:, None])
    g2, b2 = ln2_g[DEPTH - 1].reshape(1, -1), ln2_b[DEPTH - 1].reshape(1, -1)
    y_prompt = _final_ln(s2, g2, b2, TM, T_PROMPT // TM, 0).reshape(BATCH, SEQ, D_MODEL)
    y_sample = _final_ln(s2, g2, b2, DEC_BATCH, 1, T_PROMPT // DEC_BATCH).reshape(DEC_BATCH, 1, D_MODEL)
    return (y_prompt, y_sample) + tuple(jnp.stack(o) for o in outs)
```

```python
import functools
import math

import numpy as np
import jax
import jax.numpy as jnp
from jax import lax
from jax.experimental import pallas as pl
from jax.experimental.pallas import tpu as pltpu

D_MODEL = 1024
BATCH = 2
SEQ = 8192
DEPTH = 2
DEC_BATCH = 128
PAST_LEN = 16384
W_A = 256
H_A = 4
HD_A = 64
CHUNK = 128
W_B = 512
HEAD_DIM = 64
N_HEADS = 8
KV_HEADS = 2
GQ = 4
WINDOW = 128
W_C = 256
CONV_W = 3
N_BUCKETS = 32
MAX_EXACT = 16
MAX_DIST = 128
N_EXPERTS = 64
TOP_K = 8
N_GROUPS = 8
TOPK_GROUPS = 4
GROUP_SIZE = N_EXPERTS // N_GROUPS
D_EXPERT = 256
ROUTE_SCALE = 2.5
ALPHA = (2 * DEPTH) ** 0.25
LN_EPS = 1e-5
IN_COLS = 2048

T_PROMPT = BATCH * SEQ
T_ALL = T_PROMPT + DEC_BATCH

LANES = 128
SUBLANES = 8
VMEM_LIMIT = 56 * 1024 * 1024

TM = 512
NBLK = TM // WINDOW
TILES_PER_SEQ = SEQ // TM
ROWS_F32 = D_MODEL // LANES
ROWS_PK = ROWS_F32 // 2
N_TGROUPS = 4
TG = T_ALL // N_TGROUPS
RB = 256
SUB = 128
NEG = -1e30

BF = jnp.bfloat16
F32 = jnp.float32


def _ln(x, g, b):
    mu = jnp.mean(x, -1, keepdims=True)
    xc = x - mu
    var = jnp.mean(xc * xc, -1, keepdims=True)
    return xc * lax.rsqrt(var + LN_EPS) * g + b


def _rms(x, g):
    return x * lax.rsqrt(jnp.mean(x * x, -1, keepdims=True) + LN_EPS) * g


def _dot(a, b):
    return jnp.dot(a, b, preferred_element_type=F32)


def _dot_nt(a, b):
    return lax.dot_general(a, b, (((1,), (1,)), ((), ())), preferred_element_type=F32)


def _pack_pairs(lo, hi):
    lo_u = pltpu.bitcast(lo.astype(BF).astype(F32), jnp.uint32)
    hi_u = pltpu.bitcast(hi.astype(BF).astype(F32), jnp.uint32)
    return (hi_u & jnp.uint32(0xFFFF0000)) | (lo_u >> 16)


def _route(logits_t, rbias):
    n = logits_t.shape[1]
    scores = jax.nn.sigmoid(logits_t)
    biased = scores + rbias
    g3 = biased.reshape(N_GROUPS, GROUP_SIZE, n)
    sub = lax.broadcasted_iota(jnp.int32, g3.shape, 1)
    m1 = jnp.max(g3, axis=1, keepdims=True)
    first = jnp.min(jnp.where(g3 == m1, sub, GROUP_SIZE), axis=1, keepdims=True)
    m2 = jnp.max(jnp.where(sub == first, -jnp.inf, g3), axis=1, keepdims=True)
    gs = (m1 + m2).reshape(N_GROUPS, n)
    gi = lax.broadcasted_iota(jnp.int32, gs.shape, 0)
    grank = jnp.zeros(gs.shape, jnp.int32)
    for j in range(N_GROUPS):
        row = gs[j:j + 1]
        grank = grank + ((row > gs) | ((row == gs) & (j < gi))).astype(jnp.int32)
    gsel = (grank < TOPK_GROUPS).reshape(N_GROUPS, 1, n)
    masked = jnp.where(gsel, g3, -jnp.inf).reshape(N_EXPERTS, n)
    ei = lax.broadcasted_iota(jnp.int32, masked.shape, 0)
    rank = jnp.zeros(masked.shape, jnp.int32)
    for j in range(N_EXPERTS):
        row = masked[j:j + 1]
        rank = rank + ((row > masked) | ((row == masked) & (j < ei))).astype(jnp.int32)
    sel = rank < TOP_K
    w = jnp.where(sel, scores, 0.0)
    return w / jnp.sum(w, axis=0, keepdims=True) * ROUTE_SCALE


def _post_mixers(x, ya, yb, yc, mg_ref, wout_ref, ln1g_ref, ln1b_ref, wrt_ref, rb_ref,
                 swgu_ref, swd_ref, base_ref, xp_ref, gw_ref):
    n = x.shape[0]
    mg = mg_ref[...]
    cat = jnp.concatenate([_rms(ya, mg[:, :W_A]), _rms(yb, mg[:, W_A:W_A + W_B]),
                           _rms(yc, mg[:, W_A + W_B:])], axis=-1)
    m = _dot(cat.astype(BF), wout_ref[...])
    x1 = _ln(ALPHA * x + m, ln1g_ref[...], ln1b_ref[...])
    x1b = x1.astype(BF)
    gw_ref[...] = _route(_dot_nt(wrt_ref[...], x1b), rb_ref[...])
    hg = _dot(x1b, swgu_ref[...])
    hs = jax.nn.silu(hg[:, :D_EXPERT]) * hg[:, D_EXPERT:]
    base = ALPHA * x1 + _dot(hs.astype(BF), swd_ref[...])
    for c in range(ROWS_F32):
        base_ref[pl.ds(c, n, stride=ROWS_F32), :] = base[:, c * LANES:(c + 1) * LANES]
    half = D_MODEL // 2
    for j in range(ROWS_PK):
        xp_ref[pl.ds(j, n, stride=ROWS_PK), :] = _pack_pairs(
            x1[:, j * LANES:(j + 1) * LANES], x1[:, half + j * LANES:half + (j + 1) * LANES])


def _load_tokens(first_layer, xin_ref, n, lng_ref, lnb_ref):
    if first_layer:
        return xin_ref[...]
    s = jnp.concatenate([xin_ref[pl.ds(c, n, stride=ROWS_F32), :] for c in range(ROWS_F32)], axis=-1)
    return _ln(s, lng_ref[...], lnb_ref[...])


def _head_mask_sum(r):
    lane = lax.broadcasted_iota(jnp.int32, (CHUNK, W_A), 1) // HD_A
    out = r[0:CHUNK]
    for h in range(1, H_A):
        out = jnp.where(lane == h, r[h * CHUNK:(h + 1) * CHUNK], out)
    return out


def _prompt_mixer_kernel(first_layer, sinks_ref, *refs):
    (xin_ref, lng_ref, lnb_ref, win_ref, wout_ref, sguw_ref,
     sgub_ref, lnvg_ref, lnvb_ref, bprev_ref, bcur_ref, convw_ref, mg_ref, ln1g_ref,
     ln1b_ref, wrt_ref, rb_ref, swgu_ref, swd_ref,
     base_ref, xp_ref, gw_ref, klast_ref, vlast_ref, ctail_ref, vrows_ref,
     kprev_s, vprev_s, hcprev_s) = refs[(3 if first_layer else 2):]
    t = pl.program_id(1)
    is_first = t == 0

    @pl.when(is_first)
    def _():
        kprev_s[...] = jnp.zeros_like(kprev_s)
        vprev_s[...] = jnp.zeros_like(vprev_s)
        hcprev_s[...] = jnp.zeros_like(hcprev_s)

    x = _load_tokens(first_layer, xin_ref, TM, lng_ref, lnb_ref)
    z = _dot(x.astype(BF), win_ref[...])
    ua = jax.nn.gelu(z[:, 0:W_A])
    vn = _ln(jax.nn.gelu(z[:, W_A:2 * W_A]), lnvg_ref[...], lnvb_ref[...])
    q = z[:, 512:1024]
    k = z[:, 1024:1152]
    v = z[:, 1152:1280]
    gate_b = z[:, 1280:1536]
    gate_c = z[:, 1536:1792]
    h = z[:, 1792:2048]

    vnb = vn.astype(BF)
    sgub = sgub_ref[...]
    mixed = [_head_mask_sum(_dot(sguw_ref[...], vnb[c * CHUNK:(c + 1) * CHUNK])) + sgub for c in range(NBLK)]
    ya = ua * jnp.concatenate(mixed, axis=0)

    kb = k.astype(BF)
    vb = v.astype(BF)
    qb = q.astype(BF)
    scale = HEAD_DIM ** -0.5
    yb_blocks = []
    for c in range(NBLK):
        if c == 0:
            kp, vp = kprev_s[...], vprev_s[...]
            bsel = jnp.where(is_first, 1, 0)
        else:
            kp, vp = kb[(c - 1) * WINDOW:c * WINDOW], vb[(c - 1) * WINDOW:c * WINDOW]
            bsel = 0
        kc, vc = kb[c * WINDOW:(c + 1) * WINDOW], vb[c * WINDOW:(c + 1) * WINDOW]
        qc = qb[c * WINDOW:(c + 1) * WINDOW]
        outs = [None] * N_HEADS
        for g in range(KV_HEADS):
            gs = slice(g * HEAD_DIM, (g + 1) * HEAD_DIM)
            qg = jnp.concatenate([qc[:, (2 * r + g) * HEAD_DIM:(2 * r + g + 1) * HEAD_DIM] for r in range(GQ)], axis=0)
            sp = _dot_nt(qg, kp[:, gs]) * scale + bprev_ref[bsel, g]
            sc = _dot_nt(qg, kc[:, gs]) * scale + bcur_ref[g]
            for r in range(GQ):
                rs = slice(r * WINDOW, (r + 1) * WINDOW)
                sink = sinks_ref[g * GQ + r]
                spr, scr = sp[rs], sc[rs]
                mx = jnp.maximum(jnp.maximum(jnp.max(spr, -1, keepdims=True), jnp.max(scr, -1, keepdims=True)), sink)
                ep, ec = jnp.exp(spr - mx), jnp.exp(scr - mx)
                den = jnp.sum(ep, -1, keepdims=True) + jnp.sum(ec, -1, keepdims=True) + jnp.exp(sink - mx)
                o = _dot((ep / den).astype(BF), vp[:, gs]) + _dot((ec / den).astype(BF), vc[:, gs])
                outs[2 * r + g] = o
        yb_blocks.append(jnp.concatenate(outs, axis=-1))
    yb = jnp.concatenate(yb_blocks, axis=0)
    kprev_s[...] = kb[TM - WINDOW:]
    vprev_s[...] = vb[TM - WINDOW:]

    hc = gate_c * h
    prev = hcprev_s[...]
    row = lax.broadcasted_iota(jnp.int32, hc.shape, 0)
    h1 = jnp.where(row == 0, prev[7:8], pltpu.roll(hc, 1, 0))
    h2 = jnp.where(row == 0, prev[6:7], jnp.where(row == 1, prev[7:8], pltpu.roll(hc, 2, 0)))
    cw = convw_ref[...]
    yc = gate_b * (h2 * cw[0:1] + h1 * cw[1:2] + hc * cw[2:3])
    hcprev_s[...] = hc[TM - SUBLANES:]

    @pl.when(t == TILES_PER_SEQ - 1)
    def _():
        klast_ref[0] = k[TM - WINDOW:]
        vlast_ref[0] = v[TM - WINDOW:]
        ctail_ref[0] = hc[TM - (CONV_W - 1):]
        vrows_ref[0] = vn[TM - CHUNK:]

    _post_mixers(x, ya, yb, yc, mg_ref, wout_ref, ln1g_ref, ln1b_ref, wrt_ref, rb_ref, swgu_ref, swd_ref,
                 base_ref, xp_ref, gw_ref)


def _full(shape):
    nd = len(shape)
    return pl.BlockSpec(shape, lambda *_: (0,) * nd)


def _prompt_mixer(first_layer, xin, p, donors):
    nt = TILES_PER_SEQ
    tok = lambda b, t, *_: (b * nt + t, 0)
    anyspec = pl.BlockSpec(memory_space=pl.ANY)
    if first_layer:
        xin_spec = pl.BlockSpec((TM, D_MODEL), tok)
        aliases = {1: 0, 2: 1, 3: 2}
    else:
        xin_spec = pl.BlockSpec((TM * ROWS_F32, LANES), tok)
        aliases = {3: 0, 1: 1, 2: 2}
    in_specs = [anyspec] * len(donors) + [
        xin_spec, _full((1, D_MODEL)), _full((1, D_MODEL)),
        _full((D_MODEL, IN_COLS)), _full((D_MODEL, D_MODEL)), _full((H_A * CHUNK, CHUNK)),
        _full((CHUNK, W_A)), _full((1, W_A)), _full((1, W_A)),
        _full((2, KV_HEADS, GQ * WINDOW, WINDOW)), _full((KV_HEADS, GQ * WINDOW, WINDOW)),
        _full((CONV_W, W_C)), _full((1, D_MODEL)), _full((1, D_MODEL)), _full((1, D_MODEL)),
        _full((N_EXPERTS, D_MODEL)), _full((N_EXPERTS, 1)), _full((D_MODEL, 2 * D_EXPERT)),
        _full((D_EXPERT, D_MODEL)),
    ]
    per_seq = lambda b, t, *_: (b, 0, 0)
    out_shape = [
        jax.ShapeDtypeStruct((T_ALL * ROWS_F32, LANES), F32),
        jax.ShapeDtypeStruct((T_ALL * ROWS_PK, LANES), jnp.uint32),
        jax.ShapeDtypeStruct((N_EXPERTS, T_ALL), F32),
        jax.ShapeDtypeStruct((BATCH, WINDOW, KV_HEADS * HEAD_DIM), F32),
        jax.ShapeDtypeStruct((BATCH, WINDOW, KV_HEADS * HEAD_DIM), F32),
        jax.ShapeDtypeStruct((BATCH, CONV_W - 1, W_C), F32),
        jax.ShapeDtypeStruct((BATCH, CHUNK, W_A), F32),
    ]
    out_specs = [
        pl.BlockSpec((TM * ROWS_F32, LANES), tok),
        pl.BlockSpec((TM * ROWS_PK, LANES), tok),
        pl.BlockSpec((N_EXPERTS, TM), lambda b, t, *_: (0, b * nt + t)),
        pl.BlockSpec((1, WINDOW, KV_HEADS * HEAD_DIM), per_seq),
        pl.BlockSpec((1, WINDOW, KV_HEADS * HEAD_DIM), per_seq),
        pl.BlockSpec((1, CONV_W - 1, W_C), per_seq),
        pl.BlockSpec((1, CHUNK, W_A), per_seq),
    ]
    return pl.pallas_call(
        functools.partial(_prompt_mixer_kernel, first_layer),
        grid_spec=pltpu.PrefetchScalarGridSpec(
            num_scalar_prefetch=1, grid=(BATCH, nt), in_specs=in_specs, out_specs=out_specs,
            scratch_shapes=[pltpu.VMEM((WINDOW, KV_HEADS * HEAD_DIM), BF),
                            pltpu.VMEM((WINDOW, KV_HEADS * HEAD_DIM), BF),
                            pltpu.VMEM((SUBLANES, W_C), F32)]),
        out_shape=out_shape,
        input_output_aliases=aliases,
        compiler_params=pltpu.CompilerParams(dimension_semantics=("arbitrary", "arbitrary"),
                                             vmem_limit_bytes=VMEM_LIMIT),
        name="prompt_mixer",
    )(p["sinks"], *donors, xin, p["ln_prev_g"], p["ln_prev_b"], p["w_in"], p["w_out"], p["sgu_w"], p["sgu_b"],
      p["ln_v_g"], p["ln_v_b"], p["bias_prev"], p["bias_cur"], p["conv_w"], p["merge_g"], p["ln1_g"],
      p["ln1_b"], p["router_wt"], p["router_b"], p["sw_gu"], p["sw_d"])


def _half_sums(a, lane_lo):
    s0 = jnp.sum(jnp.where(lane_lo, a, 0.0), axis=1, keepdims=True)
    s1 = jnp.sum(jnp.where(lane_lo, 0.0, a), axis=1, keepdims=True)
    return jnp.where(lane_lo, s0, s1)


def _sample_mixer_kernel(first_layer, *refs):
    (xin_ref, lng_ref, lnb_ref, win_ref, wout_ref,
     sguw0_ref, sgub0_ref, lnvg_ref, lnvb_ref, ck_ref, cv_ref, bsamp_ref, bnew_ref,
     sinkrow_ref, s0_ref, s1_ref, convw_ref, mg_ref, ln1g_ref, ln1b_ref, wrt_ref,
     rb_ref, swgu_ref, swd_ref,
     base_ref, xp_ref, gw_ref, knew_ref, vnew_ref, hc_ref, vn_ref,
     q_s, yb_s, kn_s, vn_s) = refs[(3 if first_layer else 2):]
    n = DEC_BATCH
    x = _load_tokens(first_layer, xin_ref, n, lng_ref, lnb_ref)
    z = _dot(x.astype(BF), win_ref[...])
    ua = jax.nn.gelu(z[:, 0:W_A])
    vn = _ln(jax.nn.gelu(z[:, W_A:2 * W_A]), lnvg_ref[...], lnvb_ref[...])
    vn_ref[...] = vn
    ya = ua * (vn.astype(BF).astype(F32) * sguw0_ref[...] + sgub0_ref[...])
    k = z[:, 1024:1152]
    v = z[:, 1152:1280]
    knew_ref[...] = k
    vnew_ref[...] = v
    q_s[...] = z[:, 512:1024].astype(BF).astype(F32)
    kn_s[...] = k.astype(BF).astype(F32)
    vn_s[...] = v.astype(BF).astype(F32)
    scale = HEAD_DIM ** -0.5
    lane_lo = lax.broadcasted_iota(jnp.int32, (1, LANES), 1) < HEAD_DIM

    def per_seq_tile(bi, carry):
        b0 = pl.multiple_of(bi * SUBLANES, SUBLANES)
        q8 = q_s[pl.ds(b0, SUBLANES), :]
        kn8 = kn_s[pl.ds(b0, SUBLANES), :]
        vn8 = vn_s[pl.ds(b0, SUBLANES), :]
        rows = []
        for u in range(SUBLANES):
            kb = ck_ref[b0 + u].astype(F32)
            vb = cv_ref[b0 + u].astype(F32)
            kn = kn8[u:u + 1]
            vnw = vn8[u:u + 1]
            pieces = []
            for r in range(GQ):
                qp = q8[u:u + 1, r * LANES:(r + 1) * LANES]
                s = _half_sums(kb * qp, lane_lo) * scale + bsamp_ref[r]
                sn = _half_sums(kn * qp, lane_lo) * scale + bnew_ref[r]
                sink = sinkrow_ref[r]
                mx = jnp.maximum(jnp.maximum(jnp.max(s, axis=0, keepdims=True), sn), sink)
                e = jnp.exp(s - mx)
                en = jnp.exp(sn - mx)
                den = jnp.sum(e, axis=0, keepdims=True) + en + jnp.exp(sink - mx)
                p = (e / den).astype(BF).astype(F32)
                pn = (en / den).astype(BF).astype(F32)
                pieces.append(jnp.sum(p * vb, axis=0, keepdims=True) + pn * vnw)
            rows.append(jnp.concatenate(pieces, axis=1))
        yb_s[pl.ds(b0, SUBLANES), :] = jnp.concatenate(rows, axis=0)
        return carry

    lax.fori_loop(0, n // SUBLANES, per_seq_tile, 0)
    yb = yb_s[...]

    hc = z[:, 1536:1792] * z[:, 1792:2048]
    hc_ref[...] = hc
    cw = convw_ref[...]
    yc = z[:, 1280:1536] * (s0_ref[...] * cw[0:1] + s1_ref[...] * cw[1:2] + hc * cw[2:3])
    _post_mixers(x, ya, yb, yc, mg_ref, wout_ref, ln1g_ref, ln1b_ref, wrt_ref, rb_ref, swgu_ref, swd_ref,
                 base_ref, xp_ref, gw_ref)


def _sample_mixer(first_layer, xin, base, xp, gw, p, ck, cv, s0, s1):
    n = DEC_BATCH
    tail = T_PROMPT // n
    anyspec = pl.BlockSpec(memory_space=pl.ANY)
    if first_layer:
        xin_spec = pl.BlockSpec((n, D_MODEL), lambda i: (0, 0))
        lead, lead_specs, aliases = (base, xp, gw, xin), [anyspec] * 3, {0: 0, 1: 1, 2: 2}
    else:
        xin_spec = pl.BlockSpec((n * ROWS_F32, LANES), lambda i: (tail, 0))
        lead, lead_specs, aliases = (xp, gw, base), [anyspec] * 2, {2: 0, 0: 1, 1: 2}
    in_specs = lead_specs + [
        xin_spec, _full((1, D_MODEL)), _full((1, D_MODEL)),
        _full((D_MODEL, IN_COLS)), _full((D_MODEL, D_MODEL)), _full((1, W_A)), _full((1, W_A)),
        _full((1, W_A)), _full((1, W_A)),
        _full((n, WINDOW, LANES)), _full((n, WINDOW, LANES)),
        _full((GQ, WINDOW, LANES)), _full((GQ, 1, LANES)), _full((GQ, 1, LANES)),
        _full((n, W_C)), _full((n, W_C)),
        _full((CONV_W, W_C)), _full((1, D_MODEL)), _full((1, D_MODEL)), _full((1, D_MODEL)),
        _full((N_EXPERTS, D_MODEL)), _full((N_EXPERTS, 1)), _full((D_MODEL, 2 * D_EXPERT)),
        _full((D_EXPERT, D_MODEL)),
    ]
    out_shape = [
        jax.ShapeDtypeStruct(base.shape, base.dtype),
        jax.ShapeDtypeStruct(xp.shape, xp.dtype),
        jax.ShapeDtypeStruct(gw.shape, gw.dtype),
        jax.ShapeDtypeStruct((n, LANES), F32),
        jax.ShapeDtypeStruct((n, LANES), F32),
        jax.ShapeDtypeStruct((n, W_C), F32),
        jax.ShapeDtypeStruct((n, W_A), F32),
    ]
    out_specs = [
        pl.BlockSpec((n * ROWS_F32, LANES), lambda i: (tail, 0)),
        pl.BlockSpec((n * ROWS_PK, LANES), lambda i: (tail, 0)),
        pl.BlockSpec((N_EXPERTS, n), lambda i: (0, tail)),
        _full((n, LANES)), _full((n, LANES)), _full((n, W_C)), _full((n, W_A)),
    ]
    return pl.pallas_call(
        functools.partial(_sample_mixer_kernel, first_layer),
        grid=(1,), in_specs=in_specs, out_specs=out_specs, out_shape=out_shape,
        scratch_shapes=[pltpu.VMEM((n, W_B), F32), pltpu.VMEM((n, W_B), F32),
                        pltpu.VMEM((n, LANES), F32), pltpu.VMEM((n, LANES), F32)],
        input_output_aliases=aliases,
        compiler_params=pltpu.CompilerParams(dimension_semantics=("arbitrary",),
                                             vmem_limit_bytes=VMEM_LIMIT),
        name="sample_mixer",
    )(*lead, p["ln_prev_g"], p["ln_prev_b"], p["w_in"], p["w_out"], p["sgu_w0"], p["sgu_b0"],
      p["ln_v_g"], p["ln_v_b"], ck, cv, p["bias_samp"], p["bias_new"], p["sink_row"], s0, s1,
      p["conv_w"], p["merge_g"], p["ln1_g"], p["ln1_b"], p["router_wt"], p["router_b"], p["sw_gu"],
      p["sw_d"])


LIST_LEN = -(-(TG * TOP_K + RB) // 1024) * 1024
UNROLL = 8
GATHER_UNROLL = 16
INLINE_TRIPS = 16


def _expert_kernel(off_ref, tok_ref, gate_ref, xp_ref, wg_ref, wu_ref, wdn_ref, base_any, out_any,
                   acc, xs, xbf, ybuf, wgu_s, wd_s, sem):
    g = pl.program_id(0)
    e = pl.program_id(1)
    rows = TG * ROWS_F32

    def group_copy(to_vmem):
        hbm = (base_any if to_vmem else out_any).at[pl.ds(g * rows, rows)]
        vm = acc.at[pl.ds(0, rows)]
        return pltpu.make_async_copy(hbm, vm, sem) if to_vmem else pltpu.make_async_copy(vm, hbm, sem)

    @pl.when(e == 0)
    def _():
        cp = group_copy(True)
        cp.start()
        acc[pl.ds(rows, ROWS_F32), :] = jnp.zeros((ROWS_F32, LANES), F32)
        xs[...] = jnp.zeros_like(xs)
        cp.wait()

    wgu_s[:, :D_EXPERT] = wg_ref[0, 0].astype(BF)
    wgu_s[:, D_EXPERT:] = wu_ref[0, 0].astype(BF)
    wd_s[...] = wdn_ref[0, 0].astype(BF)

    start = off_ref[g * (N_EXPERTS + 1) + e]
    end = off_ref[g * (N_EXPERTS + 1) + e + 1]
    nblocks = (end - start + RB - 1) // RB

    def block(bi, carry):
        r0 = start + bi * RB
        nrows = jnp.minimum(end - r0, RB)
        ntrips = (nrows + UNROLL - 1) // UNROLL

        def gather(i, c):
            for u in range(GATHER_UNROLL):
                r = i * GATHER_UNROLL + u
                src = pl.multiple_of(tok_ref[r0 + r] * ROWS_PK, ROWS_PK)
                dst = r * ROWS_PK if isinstance(r, int) else pl.multiple_of(r * ROWS_PK, ROWS_PK)
                xs[pl.ds(dst, ROWS_PK), :] = xp_ref[pl.ds(src, ROWS_PK), :]
            return c

        lax.fori_loop(0, (jnp.minimum(nrows, SUB) + GATHER_UNROLL - 1) // GATHER_UNROLL, gather, 0)
        half = D_MODEL // 2

        def sub_block(lo):
            for j in range(ROWS_PK):
                u32 = xs[pl.ds(lo * ROWS_PK + j, SUB, stride=ROWS_PK), :]
                xbf[lo:lo + SUB, j * LANES:(j + 1) * LANES] = pltpu.bitcast(u32 << 16, F32).astype(BF)
                xbf[lo:lo + SUB, half + j * LANES:half + (j + 1) * LANES] = pltpu.bitcast(
                    u32 & jnp.uint32(0xFFFF0000), F32).astype(BF)
            hg = _dot(xbf[lo:lo + SUB, :], wgu_s[...])
            hid = jax.nn.silu(hg[:, :D_EXPERT]) * hg[:, D_EXPERT:]
            y = _dot(hid.astype(BF), wd_s[...])
            for c in range(ROWS_F32):
                ybuf[pl.ds(lo * ROWS_F32 + c, SUB, stride=ROWS_F32), :] = y[:, c * LANES:(c + 1) * LANES]

        def scatter(checked, i, c):
            dsts, vals = [], []
            for u in range(UNROLL):
                r = i * UNROLL + u
                tk, gt = tok_ref[r0 + r], gate_ref[r0 + r]
                if checked:
                    valid = r < nrows
                    tk, gt = jnp.where(valid, tk, TG), jnp.where(valid, gt, 0.0)
                dst = pl.multiple_of(tk * ROWS_F32, ROWS_F32)
                src = r * ROWS_F32 if isinstance(r, int) else pl.multiple_of(r * ROWS_F32, ROWS_F32)
                dsts.append(dst)
                vals.append(acc[pl.ds(dst, ROWS_F32), :] + gt * ybuf[pl.ds(src, ROWS_F32), :])
            for dst, val in zip(dsts, vals):
                acc[pl.ds(dst, ROWS_F32), :] = val
            return c

        def two_sub_blocks():
            for i in range(SUB // GATHER_UNROLL, RB // GATHER_UNROLL):
                gather(i, 0)
            sub_block(0)
            for i in range(INLINE_TRIPS):
                scatter(False, i, 0)
            sub_block(SUB)

        lax.cond(nrows > SUB, two_sub_blocks, lambda: sub_block(0))

        nfull = nrows // UNROLL
        first_trip = jnp.where(nrows > SUB, INLINE_TRIPS, 0)
        lax.fori_loop(first_trip, nfull, functools.partial(scatter, False), 0)
        lax.fori_loop(nfull, ntrips, functools.partial(scatter, True), 0)
        return carry

    lax.fori_loop(0, nblocks, block, 0)

    @pl.when(e == N_EXPERTS - 1)
    def _():
        cp = group_copy(False)
        cp.start()
        cp.wait()


def _experts(layer, off, tok, gate, xp, wg, wu, wdn, base):
    lists = pl.BlockSpec((LIST_LEN,), lambda g, e, *_: (g,), memory_space=pltpu.SMEM)
    anyspec = pl.BlockSpec(memory_space=pl.ANY)
    in_specs = [
        lists, lists,
        pl.BlockSpec((TG * ROWS_PK, LANES), lambda g, e, *_: (g, 0)),
        pl.BlockSpec((1, 1, D_MODEL, D_EXPERT), lambda g, e, *_: (layer, e, 0, 0)),
        pl.BlockSpec((1, 1, D_MODEL, D_EXPERT), lambda g, e, *_: (layer, e, 0, 0)),
        pl.BlockSpec((1, 1, D_EXPERT, D_MODEL), lambda g, e, *_: (layer, e, 0, 0)),
        anyspec,
    ]
    return pl.pallas_call(
        _expert_kernel,
        grid_spec=pltpu.PrefetchScalarGridSpec(
            num_scalar_prefetch=1, grid=(N_TGROUPS, N_EXPERTS), in_specs=in_specs,
            out_specs=anyspec,
            scratch_shapes=[pltpu.VMEM(((TG + 1) * ROWS_F32, LANES), F32),
                            pltpu.VMEM((RB * ROWS_PK, LANES), jnp.uint32),
                            pltpu.VMEM((RB, D_MODEL), BF),
                            pltpu.VMEM((RB * ROWS_F32, LANES), F32),
                            pltpu.VMEM((D_MODEL, 2 * D_EXPERT), BF),
                            pltpu.VMEM((D_EXPERT, D_MODEL), BF),
                            pltpu.SemaphoreType.DMA]),
        out_shape=jax.ShapeDtypeStruct(base.shape, base.dtype),
        input_output_aliases={7: 0},
        compiler_params=pltpu.CompilerParams(dimension_semantics=("arbitrary", "arbitrary"),
                                             vmem_limit_bytes=VMEM_LIMIT),
        name="experts",
    )(off, tok, gate, xp, wg, wu, wdn, base)


def _dispatch_lists(gw):
    chosen = (gw > 0).reshape(N_EXPERTS, N_TGROUPS, TG).transpose(1, 0, 2)
    counts = jnp.sum(chosen, axis=-1, dtype=jnp.int32)
    off = jnp.concatenate([jnp.zeros((N_TGROUPS, 1), jnp.int32), jnp.cumsum(counts, axis=1)], axis=1)
    flat = chosen.reshape(N_TGROUPS, N_EXPERTS * TG)
    idx = jax.vmap(lambda f: jnp.nonzero(f, size=TG * TOP_K, fill_value=0)[0])(flat).astype(jnp.int32)
    gflat = gw.reshape(N_EXPERTS, N_TGROUPS, TG).transpose(1, 0, 2).reshape(N_TGROUPS, N_EXPERTS * TG)
    gate = jnp.take_along_axis(gflat, idx, axis=1)
    pad = ((0, 0), (0, LIST_LEN - TG * TOP_K))
    tok = jnp.pad(idx % TG, pad).reshape(-1)
    return off.reshape(-1), tok, jnp.pad(gate, pad).reshape(-1)


def _final_ln_kernel(n, s_ref, g_ref, b_ref, o_ref):
    o_ref[...] = _load_tokens(False, s_ref, n, g_ref, b_ref)


def _final_ln(s2, g, b, n, steps, first_block):
    return pl.pallas_call(
        functools.partial(_final_ln_kernel, n),
        grid=(steps,),
        in_specs=[pl.BlockSpec((n * ROWS_F32, LANES), lambda i: (first_block + i, 0)),
                  _full((1, D_MODEL)), _full((1, D_MODEL))],
        out_specs=pl.BlockSpec((n, D_MODEL), lambda i: (i, 0)),
        out_shape=jax.ShapeDtypeStruct((n * steps, D_MODEL), F32),
        compiler_params=pltpu.CompilerParams(dimension_semantics=("arbitrary",)),
        name="final_norm",
    )(s2, g, b)


_HEAD_AT = [(p % 2) * GQ + p // 2 for p in range(N_HEADS)]


def _bucket(dist):
    n = np.maximum(dist, 0)
    nf = np.maximum(n, 1).astype(np.float32)
    large = MAX_EXACT + (np.log(nf / np.float32(MAX_EXACT)) / np.float32(math.log(MAX_DIST / MAX_EXACT))
                         * np.float32(N_BUCKETS - MAX_EXACT)).astype(np.int32)
    return np.where(n < MAX_EXACT, n, np.minimum(large, N_BUCKETS - 1)).astype(np.int32)


def _bias_tables(rel_bias):
    qi = np.arange(WINDOW)[:, None]
    kk = np.arange(WINDOW)[None, :]

    def table(dist):
        valid = (dist >= 0) & (dist < WINDOW)
        t = jnp.where(valid[..., None], rel_bias[_bucket(dist)], NEG)
        return jnp.moveaxis(t, -1, 0).reshape(KV_HEADS, GQ * WINDOW, WINDOW)

    prev = table(WINDOW + qi - kk)
    bias_prev = jnp.stack([prev, jnp.full_like(prev, NEG)])
    bias_cur = table(qi - kk)
    dist_s = WINDOW - np.arange(WINDOW)
    col = jnp.where(((dist_s >= 0) & (dist_s < WINDOW))[:, None], rel_bias[_bucket(dist_s)], NEG)
    pairs = [jnp.concatenate([jnp.broadcast_to(col[:, r:r + 1], (WINDOW, HEAD_DIM)),
                              jnp.broadcast_to(col[:, GQ + r:GQ + r + 1], (WINDOW, HEAD_DIM))], axis=1)
             for r in range(GQ)]
    new = rel_bias[0]
    bias_new = jnp.stack([_pair_row(new, r) for r in range(GQ)])
    return bias_prev, bias_cur, jnp.stack(pairs), bias_new


def _pair_row(per_head, r):
    return jnp.concatenate([jnp.broadcast_to(per_head[r], (1, HEAD_DIM)),
                            jnp.broadcast_to(per_head[GQ + r], (1, HEAD_DIM))], axis=1)


def _prep_layer(l, tables, w_in, w_out, ln_v_g, ln_v_b, sgu_w, sgu_b, attn_sinks, conv_w, merge_g,
                ln1_g, ln1_b, ln2_g, ln2_b, router_w, router_bias, sw_gate, sw_up, sw_down):
    q0 = 2 * W_A
    qcols = np.concatenate([q0 + h * HEAD_DIM + np.arange(HEAD_DIM) for h in _HEAD_AT])
    cols = np.concatenate([np.arange(q0), qcols, np.arange(q0 + W_B, IN_COLS)])
    rows = np.concatenate([np.arange(W_A), qcols - q0 + W_A, np.arange(W_A + W_B, D_MODEL)])
    row = lambda a: a.reshape(1, -1)
    prev = max(l - 1, 0)
    bias_prev, bias_cur, bias_samp, bias_new = tables
    return dict(
        sinks=attn_sinks[l],
        ln_prev_g=row(ln2_g[prev]), ln_prev_b=row(ln2_b[prev]),
        w_in=w_in[l][:, cols].astype(BF), w_out=w_out[l][rows].astype(BF),
        sgu_w=jnp.tril(sgu_w[l]).reshape(H_A * CHUNK, CHUNK).astype(BF),
        sgu_b=jnp.repeat(sgu_b[l].T, HD_A, axis=1),
        sgu_w0=row(jnp.repeat(sgu_w[l][:, 0, 0].astype(BF).astype(F32), HD_A)),
        sgu_b0=row(jnp.repeat(sgu_b[l][:, 0], HD_A)),
        ln_v_g=row(ln_v_g[l]), ln_v_b=row(ln_v_b[l]),
        bias_prev=bias_prev, bias_cur=bias_cur, bias_samp=bias_samp, bias_new=bias_new,
        sink_row=jnp.stack([_pair_row(attn_sinks[l], r) for r in range(GQ)]),
        conv_w=conv_w[l].T, merge_g=row(merge_g[l][rows]), ln1_g=row(ln1_g[l]), ln1_b=row(ln1_b[l]),
        router_wt=router_w[l].T.astype(BF), router_b=router_bias[l].reshape(N_EXPERTS, 1),
        sw_gu=jnp.concatenate([sw_gate[l], sw_up[l]], axis=1).astype(BF), sw_d=sw_down[l].astype(BF),
    )


def kernel(x_prompt, x_sample, cache_k, cache_v, state_conv, rel_bias, w_in, w_out, ln_v_g, ln_v_b,
           sgu_w, sgu_b, attn_sinks, conv_w, merge_g, ln1_g, ln1_b, ln2_g, ln2_b, router_w, router_bias,
           ew_gate, ew_up, ew_down, sw_gate, sw_up, sw_down):
    tables = _bias_tables(rel_bias)
    xin_p = x_prompt.reshape(T_PROMPT, D_MODEL)
    xin_s = x_sample.reshape(DEC_BATCH, D_MODEL)
    kv_shape = (WINDOW, KV_HEADS, HEAD_DIM)
    outs = [[] for _ in range(8)]
    for l in range(DEPTH):
        p = _prep_layer(l, tables, w_in, w_out, ln_v_g, ln_v_b, sgu_w, sgu_b, attn_sinks, conv_w, merge_g,
                        ln1_g, ln1_b, ln2_g, ln2_b, router_w, router_bias, sw_gate, sw_up, sw_down)
        if l == 0:
            donors = (jnp.zeros((T_ALL * ROWS_F32, LANES), F32), jnp.zeros((T_ALL * ROWS_PK, LANES), jnp.uint32),
                      jnp.zeros((N_EXPERTS, T_ALL), F32))
        else:
            donors = (xp, gw)
        base, xp, gw, klast, vlast, ctail, vrows = _prompt_mixer(l == 0, xin_p, p, donors)
        ck = cache_k[l].astype(BF).reshape(DEC_BATCH, WINDOW, LANES)
        cv = cache_v[l].astype(BF).reshape(DEC_BATCH, WINDOW, LANES)
        s0, s1 = state_conv[l][:, 0], state_conv[l][:, 1]
        base, xp, gw, knew, vnew, hc, vns = _sample_mixer(l == 0, xin_s, base, xp, gw, p, ck, cv, s0, s1)
        off, tok, gate = _dispatch_lists(gw)
        s2 = _experts(l, off, tok, gate, xp, ew_gate, ew_up, ew_down, base)
        xin_p = xin_s = s2
        outs[0].append(klast.reshape((BATCH,) + kv_shape))
        outs[1].append(vlast.reshape((BATCH,) + kv_shape))
        outs[2].append(ctail)
        outs[3].append(vrows)
        outs[4].append(jnp.concatenate([cache_k[l][:, 1:], knew.reshape((DEC_BATCH, 1) + kv_shape[1:])], axis=1))
        outs[5].append(jnp.concatenate([cache_v[l][:, 1:], vnew.reshape((DEC_BATCH, 1) + kv_shape[1:])], axis=1))
        outs[6].append(jnp.stack([s1, hc], axis=1))
        outs[7].append(vns[:, None])
    g2, b2 = ln2_g[DEPTH - 1].reshape(1, -1), ln2_b[DEPTH - 1].reshape(1, -1)
    y_prompt = _final_ln(s2, g2, b2, TM, T_PROMPT // TM, 0).reshape(BATCH, SEQ, D_MODEL)
    y_sample = _final_ln(s2, g2, b2, DEC_BATCH, 1, T_PROMPT // DEC_BATCH).reshape(DEC_BATCH, 1, D_MODEL)
    return (y_prompt, y_sample) + tuple(jnp.stack(o) for o in outs)
```
